```python
import functools
import jax, jax.numpy as jnp
from jax import lax
import numpy as np

D_MODEL = 1024
BATCH = 4
SEQ = 8192
DEPTH = 1
DEC_BATCH = 32
DEC_SEQ = 16
PAST_LEN = 4096

CHUNK = 64
Q_BLOCK = 128
EPS = 1e-6
NEG = -1e30
A_HEADS = 8
A_HEAD_DIM = 64
A_WIDTH = A_HEADS * A_HEAD_DIM
FOX_FORGET_BIAS = 2.0
B_HEADS = 4
B_QK_DIM = 128
B_V_DIM = 128
B_WIDTH = B_HEADS * B_V_DIM
MLSTM_FORGET_BIAS = 3.0
P_HEADS = 8
P_KEYS = 128
P_EXPERTS = P_KEYS * P_KEYS
P_KEY_DIM = 128
P_HALF = P_KEY_DIM // 2
P_TOPK = 16
P_BLOCK = 256
IN_SIZES = (A_WIDTH, A_WIDTH, A_WIDTH, A_HEADS,
            B_HEADS * B_QK_DIM, B_HEADS * B_QK_DIM, B_WIDTH, B_HEADS, B_HEADS, B_WIDTH,
            D_MODEL, D_MODEL)
IN_WIDTH = 3 * A_WIDTH + A_HEADS + 2 * B_HEADS * B_QK_DIM + 2 * B_WIDTH + 2 * B_HEADS + 2 * D_MODEL

kernel_name = "fox_mlstm_peer_streaming_encoder_step"


def _col(i):
    start = sum(IN_SIZES[:i])
    return start, start + IN_SIZES[i]


def _split_points():
    pts, acc = [], 0
    for s in IN_SIZES[:-1]:
        acc += s
        pts.append(acc)
    return pts


def rms_norm(x):
    xf = x.astype(jnp.float32)
    return (xf * lax.rsqrt(jnp.mean(xf * xf, axis=-1, keepdims=True) + EPS)).astype(x.dtype)


def fox_attend(q, F_q, q_pos, k, v, F_k):
    s = jnp.einsum('blhd,bshd->bhls', q, k).astype(jnp.float32) * (A_HEAD_DIM ** -0.5)
    s = s + F_q[..., None] - F_k[:, :, None, :]
    mask = jnp.arange(k.shape[1])[None, :] <= q_pos[:, None]
    p = jax.nn.softmax(jnp.where(mask, s, NEG), axis=-1).astype(v.dtype)
    return jnp.einsum('bhls,bshd->blhd', p, v)


def fox_prompt(q, k, v, logf):
    b, s, h, d = q.shape
    F = jnp.cumsum(logf, axis=1).transpose(0, 2, 1)
    def block(i):
        start = i * Q_BLOCK
        qi = lax.dynamic_slice_in_dim(q, start, Q_BLOCK, axis=1)
        Fi = lax.dynamic_slice_in_dim(F, start, Q_BLOCK, axis=2)
        return fox_attend(qi, Fi, start + jnp.arange(Q_BLOCK), k, v, F)
    o = lax.map(block, jnp.arange(s // Q_BLOCK))
    return jnp.moveaxis(o, 0, 1).reshape(b, s, h * d)


def fox_sample(q, k, v, logf, ck, cv, clf):
    b, L, h, d = q.shape
    past = ck.shape[1]
    kk = jnp.concatenate([ck.astype(k.dtype), k], axis=1)
    vv = jnp.concatenate([cv.astype(v.dtype), v], axis=1)
    F = jnp.cumsum(jnp.concatenate([clf.astype(jnp.float32), logf], axis=1), axis=1).transpose(0, 2, 1)
    o = fox_attend(q, F[:, :, past:], past + jnp.arange(L), kk, vv, F)
    return o.reshape(b, L, h * d)


def mlstm_chunk(carry, inp):
    C0, n0, m0 = carry
    q, k, v, ig, lf = inp
    L = q.shape[1]
    bcum = jnp.cumsum(lf, axis=1).transpose(0, 2, 1)
    it = ig.transpose(0, 2, 1)
    causal = jnp.tril(jnp.ones((L, L), dtype=bool))
    dmat = jnp.where(causal, bcum[..., :, None] - bcum[..., None, :] + it[..., None, :], NEG)
    inter = m0[..., None] + bcum
    m = jnp.maximum(dmat.max(axis=-1), inter)
    w = jnp.exp(dmat - m[..., None])
    a = jnp.exp(inter - m)
    ws = w * jnp.einsum('bthd,bshd->bhts', q, k)
    num = jnp.einsum('bhts,bshv->bthv', ws, v) + jnp.einsum('bht,bthd,bhdv->bthv', a, q, C0)
    den = ws.sum(axis=-1) + a * jnp.einsum('bthd,bhd->bht', q, n0)
    h = num / jnp.maximum(jnp.abs(den), jnp.exp(-m)).transpose(0, 2, 1)[..., None]
    mL = m[..., -1]
    wL = jnp.exp(bcum[..., -1:] - bcum + it - mL[..., None])
    aL = jnp.exp(m0 + bcum[..., -1] - mL)
    C = aL[..., None, None] * C0 + jnp.einsum('bhs,bshd,bshv->bhdv', wL, k, v)
    n = aL[..., None] * n0 + jnp.einsum('bhs,bshd->bhd', wL, k)
    return (C, n, mL), h


def mlstm_prompt(q, k, v, ig, lf):
    b, s = q.shape[:2]
    nc = s // CHUNK
    def chunks(t):
        return jnp.swapaxes(t.reshape((b, nc, CHUNK) + t.shape[2:]), 0, 1)
    init = (jnp.zeros((b, B_HEADS, B_QK_DIM, B_V_DIM), jnp.float32),
            jnp.zeros((b, B_HEADS, B_QK_DIM), jnp.float32),
            jnp.zeros((b, B_HEADS), jnp.float32))
    state, h = lax.scan(mlstm_chunk, init, (chunks(q), chunks(k), chunks(v), chunks(ig), chunks(lf)))
    return jnp.swapaxes(h, 0, 1).reshape(b, s, B_HEADS, B_V_DIM), state


def prompt_mix(qa, ka, va, lfa, qb, kb, vb, ib, lfb):
    ha = fox_prompt(qa, ka, va, lfa)
    hb, (C, n, m) = mlstm_prompt(qb, kb, vb, ib, lfb)
    return ha, hb, (ka, va, lfa, C, n, m)


def sample_mix(ck, cv, clf, C0, n0, m0, qa, ka, va, lfa, qb, kb, vb, ib, lfb):
    ha = fox_sample(qa, ka, va, lfa, ck, cv, clf)
    carry = (C0.astype(jnp.float32), n0.astype(jnp.float32), m0.astype(jnp.float32))
    (C, n, m), hb = mlstm_chunk(carry, (qb, kb, vb, ib, lfb))
    return ha, hb, (ka, va, lfa, C, n, m)


def peer(x, w_pq, sub_keys, expert_u, expert_v):
    n = x.shape[0]
    pad = (-n) % P_BLOCK
    xb = jnp.pad(x, ((0, pad), (0, 0))).reshape(-1, P_BLOCK, D_MODEL)
    def blk(xi):
        q = (xi @ w_pq).reshape(P_BLOCK, P_HEADS, 2, P_HALF)
        s = jnp.einsum('thpd,hpkd->thpk', q, sub_keys).astype(jnp.float32)
        sv, si = lax.top_k(s, P_TOPK)
        cand = (sv[:, :, 0, :, None] + sv[:, :, 1, None, :]).reshape(P_BLOCK, P_HEADS, P_TOPK * P_TOPK)
        cidx = (si[:, :, 0, :, None] * P_KEYS + si[:, :, 1, None, :]).reshape(P_BLOCK, P_HEADS, P_TOPK * P_TOPK)
        fv, fi = lax.top_k(cand, P_TOPK)
        eidx = jnp.take_along_axis(cidx, fi, axis=-1)
        g = jax.nn.softmax(fv, axis=-1)
        act = jax.nn.gelu(jnp.einsum('thkd,td->thk', expert_u[eidx], xi).astype(jnp.float32), approximate=False)
        return jnp.einsum('thk,thkd->td', (g * act).astype(xi.dtype), expert_v[eidx])
    return lax.map(blk, xb).reshape(-1, D_MODEL)[:n]


def trunk_layer(x, c, token_mix, w_mod, b_mod, g_norm1, g_norm2, w_in, b_in, g_qa, g_ka, g_hb,
                w_pa, w_pb, w_o, w_pq, sub_keys, expert_u, expert_v):
    bsz, L, _ = x.shape
    f32 = jnp.float32
    sh1, sc1, gt1, sh2, sc2, gt2 = jnp.split((jax.nn.silu(c) @ w_mod + b_mod)[:, None, :], 6, axis=-1)
    h = rms_norm(x) * g_norm1 * (1.0 + sc1) + sh1
    z = h @ w_in + b_in
    aq, ak, av, af, bq, bk, bv, bi, bf, bo, ga, gb = jnp.split(z, _split_points(), axis=-1)
    qa = rms_norm(aq.reshape(bsz, L, A_HEADS, A_HEAD_DIM)) * g_qa
    ka = rms_norm(ak.reshape(bsz, L, A_HEADS, A_HEAD_DIM)) * g_ka
    va = av.reshape(bsz, L, A_HEADS, A_HEAD_DIM)
    lfa = jax.nn.log_sigmoid(af.astype(f32))
    qb = bq.reshape(bsz, L, B_HEADS, B_QK_DIM).astype(f32)
    kb = bk.reshape(bsz, L, B_HEADS, B_QK_DIM).astype(f32) * (B_QK_DIM ** -0.5)
    vb = bv.reshape(bsz, L, B_HEADS, B_V_DIM).astype(f32)
    ib = bi.astype(f32)
    lfb = jax.nn.log_sigmoid(bf.astype(f32))
    ha, hb, state = token_mix(qa, ka, va, lfa, qb, kb, vb, ib, lfb)
    hb = (rms_norm(hb) * g_hb).reshape(bsz, L, B_WIDTH).astype(x.dtype) * jax.nn.sigmoid(bo)
    merged = jax.nn.sigmoid(ga) * (ha @ w_pa) + jax.nn.sigmoid(gb) * (hb @ w_pb)
    x = x + gt1 * (merged @ w_o)
    h2 = rms_norm(x) * g_norm2 * (1.0 + sc2) + sh2
    y = peer(h2.reshape(bsz * L, D_MODEL), w_pq, sub_keys, expert_u, expert_v).reshape(bsz, L, D_MODEL)
    return x + gt2 * y, state


def setup_inputs(seed: int = 0) -> dict:
    key = jax.random.key(seed)
    ks = iter(jax.random.split(key, 32))
    def nrm(shape, scale):
        return jax.random.normal(next(ks), shape, jnp.float32) * scale
    D = D_MODEL
    x_prompt = nrm((BATCH, SEQ, D), 1.0)
    x_sample = nrm((DEC_BATCH, DEC_SEQ, D), 1.0)
    c_prompt = nrm((BATCH, D), 1.0)
    c_sample = nrm((DEC_BATCH, D), 1.0)
    cache_k = nrm((DEPTH, DEC_BATCH, PAST_LEN, A_HEADS, A_HEAD_DIM), 1.0)
    cache_v = nrm((DEPTH, DEC_BATCH, PAST_LEN, A_HEADS, A_HEAD_DIM), 1.0)
    cache_logf = jax.nn.log_sigmoid(FOX_FORGET_BIAS + nrm((DEPTH, DEC_BATCH, PAST_LEN, A_HEADS), 1.0))
    state_C = nrm((DEPTH, DEC_BATCH, B_HEADS, B_QK_DIM, B_V_DIM), 0.1)
    state_n = nrm((DEPTH, DEC_BATCH, B_HEADS, B_QK_DIM), 0.1)
    state_m = nrm((DEPTH, DEC_BATCH, B_HEADS), 1.0)
    w_mod = nrm((DEPTH, D, 6 * D), 0.3 * D ** -0.5)
    b_mod = nrm((DEPTH, 6 * D), 0.02)
    g_norm1 = 1.0 + nrm((DEPTH, D), 0.02)
    g_norm2 = 1.0 + nrm((DEPTH, D), 0.02)
    w_in = nrm((DEPTH, D, IN_WIDTH), D ** -0.5)
    fa0, fa1 = _col(3)
    fb0, fb1 = _col(8)
    b_in = nrm((DEPTH, IN_WIDTH), 0.02)
    b_in = b_in.at[:, fa0:fa1].add(FOX_FORGET_BIAS).at[:, fb0:fb1].add(MLSTM_FORGET_BIAS)
    g_qa = 1.0 + nrm((DEPTH, A_HEADS, A_HEAD_DIM), 0.02)
    g_ka = 1.0 + nrm((DEPTH, A_HEADS, A_HEAD_DIM), 0.02)
    g_hb = 1.0 + nrm((DEPTH, B_HEADS, B_V_DIM), 0.02)
    w_pa = nrm((DEPTH, A_WIDTH, D), A_WIDTH ** -0.5)
    w_pb = nrm((DEPTH, B_WIDTH, D), B_WIDTH ** -0.5)
    w_o = nrm((DEPTH, D, D), D ** -0.5)
    w_pq = nrm((DEPTH, D, P_HEADS * P_KEY_DIM), D ** -0.5)
    sub_keys = nrm((DEPTH, P_HEADS, 2, P_KEYS, P_HALF), P_HALF ** -0.5)
    expert_u = nrm((DEPTH, P_EXPERTS, D), D ** -0.5)
    expert_v = nrm((DEPTH, P_EXPERTS, D), P_HEADS ** -0.5)
    return {"x_prompt": x_prompt, "x_sample": x_sample, "c_prompt": c_prompt, "c_sample": c_sample,
            "cache_k": cache_k, "cache_v": cache_v, "cache_logf": cache_logf,
            "state_C": state_C, "state_n": state_n, "state_m": state_m,
            "w_mod": w_mod, "b_mod": b_mod, "g_norm1": g_norm1, "g_norm2": g_norm2,
            "w_in": w_in, "b_in": b_in, "g_qa": g_qa, "g_ka": g_ka, "g_hb": g_hb,
            "w_pa": w_pa, "w_pb": w_pb, "w_o": w_o, "w_pq": w_pq, "sub_keys": sub_keys,
            "expert_u": expert_u, "expert_v": expert_v}


def reference(x_prompt, x_sample, c_prompt, c_sample, cache_k, cache_v, cache_logf,
              state_C, state_n, state_m, w_mod, b_mod, g_norm1, g_norm2, w_in, b_in,
              g_qa, g_ka, g_hb, w_pa, w_pb, w_o, w_pq, sub_keys, expert_u, expert_v):
    y_p, y_s = x_prompt, x_sample
    new_p = [[] for _ in range(6)]
    new_s = [[] for _ in range(6)]
    for l in range(DEPTH):
        weights = (w_mod[l], b_mod[l], g_norm1[l], g_norm2[l], w_in[l], b_in[l], g_qa[l], g_ka[l], g_hb[l],
                   w_pa[l], w_pb[l], w_o[l], w_pq[l], sub_keys[l], expert_u[l], expert_v[l])
        y_p, st_p = trunk_layer(y_p, c_prompt, prompt_mix, *weights)
        mix_s = functools.partial(sample_mix, cache_k[l], cache_v[l], cache_logf[l],
                                  state_C[l], state_n[l], state_m[l])
        y_s, st_s = trunk_layer(y_s, c_sample, mix_s, *weights)
        for lst, arr in zip(new_p, st_p):
            lst.append(arr)
        for lst, arr in zip(new_s, st_s):
            lst.append(arr)
    return (y_p, y_s,
            jnp.stack(new_p[0]), jnp.stack(new_p[1]), jnp.stack(new_p[2]),
            jnp.stack(new_p[3]), jnp.stack(new_p[4]), jnp.stack(new_p[5]),
            jnp.stack(new_s[0]), jnp.stack(new_s[1]), jnp.stack(new_s[2]),
            jnp.stack(new_s[3]), jnp.stack(new_s[4]), jnp.stack(new_s[5]))
```

```python
import functools

import jax
import jax.numpy as jnp
from jax import lax
from jax.experimental import pallas as pl
from jax.experimental.pallas import tpu as pltpu

EPS = 1e-6
NEG = -1e30
D_MODEL = 1024
A_HEADS = 8
A_HEAD_DIM = 64
A_WIDTH = A_HEADS * A_HEAD_DIM
B_HEADS = 4
B_DIM = 128
B_WIDTH = B_HEADS * B_DIM
P_HEADS = 8
P_KEYS = 128
P_HALF = 64
P_TOPK = 16
P_SEL = P_HEADS * P_TOPK
LANES = 128
SUBLANES = 8
SMALL_W = 128
VMEM_LIMIT = 48 * 2**20
TABLE_VMEM_LIMIT = 56 * 2**20
F32 = jnp.float32
BF16 = jnp.bfloat16


def _dot(a, b):
    return jnp.dot(a, b, preferred_element_type=F32)


def _dot_nt(a, b):
    return lax.dot_general(a, b, (((1,), (1,)), ((), ())), preferred_element_type=F32)


def _dot_tn(a, b):
    return lax.dot_general(a, b, (((0,), (0,)), ((), ())), preferred_element_type=F32)


def _dot_f32(a, b):
    return jnp.dot(a, b, preferred_element_type=F32, precision=lax.Precision.HIGHEST)


def _split2(x):
    hi = x.astype(BF16)
    lo = (x - hi.astype(F32)).astype(BF16)
    return hi, lo


def _log_sigmoid(x):
    return jnp.minimum(x, 0.0) - jnp.log(1.0 + jnp.exp(-jnp.abs(x)))


def _sigmoid(x):
    return 1.0 / (1.0 + jnp.exp(-x))


def _resident(shape):
    nd = len(shape)
    return pl.BlockSpec(shape, lambda *_: (0,) * nd, pipeline_mode=pl.Buffered(1))


def _mod_kernel(c_ref, w_ref, b_ref, o_ref):
    c = c_ref[...]
    s = c * _sigmoid(c)
    o_ref[...] = _dot_f32(s, w_ref[...]) + b_ref[...]


def _modulation(c, w_mod, b_mod):
    bsz = c.shape[0]
    n = w_mod.shape[1]
    tn = 1024
    return pl.pallas_call(
        _mod_kernel,
        grid=(n // tn,),
        in_specs=[pl.BlockSpec((bsz, D_MODEL), lambda j: (0, 0)),
                  pl.BlockSpec((D_MODEL, tn), lambda j: (0, j)),
                  pl.BlockSpec((1, tn), lambda j: (0, j))],
        out_specs=pl.BlockSpec((bsz, tn), lambda j: (0, j)),
        out_shape=jax.ShapeDtypeStruct((bsz, n), F32),
        compiler_params=pltpu.CompilerParams(dimension_semantics=("arbitrary",), vmem_limit_bytes=VMEM_LIMIT),
        name="modulation",
    )(c, w_mod, b_mod.reshape(1, n))


def _inproj_kernel(x_ref, mod_ref, g1_ref, wa_ref, ba_ref, wb_ref, bb_ref, wg_ref, bg_ref,
                   wsh_ref, wsl_ref, bs_ref, gq_ref, gk_ref, bd_ref,
                   qa_ref, ka_ref, va_ref, qb_ref, kb_ref, vb_ref, ob_ref, sga_ref, sgb_ref, small_ref):
    x = x_ref[0]
    mod = mod_ref[0]
    sh1 = mod[:, 0:D_MODEL]
    sc1 = mod[:, D_MODEL:2 * D_MODEL]
    h = x * lax.rsqrt(jnp.mean(x * x, axis=-1, keepdims=True) + EPS)
    h = h * g1_ref[...] * (1.0 + sc1) + sh1
    hh, hl = _split2(h)

    za = _dot(hh, wa_ref[...]) + ba_ref[...]
    bd = bd_ref[...]

    def head_norm(z):
        msq = _dot((z * z).astype(BF16), bd)
        return z * lax.rsqrt(msq + EPS)

    q = head_norm(za[:, 0:A_WIDTH]) * gq_ref[...]
    k = head_norm(za[:, A_WIDTH:2 * A_WIDTH]) * gk_ref[...]
    qa_ref[0] = (q * (A_HEAD_DIM ** -0.5)).astype(BF16)
    ka_ref[0] = k
    va_ref[0] = za[:, 2 * A_WIDTH:3 * A_WIDTH]

    zb = _dot(hh, wb_ref[...]) + bb_ref[...]
    qb_ref[0] = zb[:, 0:B_WIDTH].astype(BF16)
    kb_ref[0] = (zb[:, B_WIDTH:2 * B_WIDTH] * (B_DIM ** -0.5)).astype(BF16)
    vb_ref[0] = zb[:, 2 * B_WIDTH:3 * B_WIDTH].astype(BF16)

    zg = _sigmoid(_dot(hh, wg_ref[...]) + bg_ref[...])
    ob_ref[0] = zg[:, 0:B_WIDTH].astype(BF16)
    sga_ref[0] = zg[:, B_WIDTH:B_WIDTH + D_MODEL].astype(BF16)
    sgb_ref[0] = zg[:, B_WIDTH + D_MODEL:B_WIDTH + 2 * D_MODEL].astype(BF16)

    zs = _dot(hh, wsh_ref[...]) + _dot(hl, wsh_ref[...]) + _dot(hh, wsl_ref[...]) + bs_ref[...]
    col = lax.broadcasted_iota(jnp.int32, zs.shape, 1)
    is_input_gate = (col >= A_HEADS) & (col < A_HEADS + B_HEADS)
    small_ref[0] = jnp.where(is_input_gate, zs, _log_sigmoid(zs))


def _inproj(x, mod, g1, wts, tm):
    bsz, seq, _ = x.shape
    (wa, ba, wb, bb, wg, bg, wsh, wsl, bs, gq, gk, bd) = wts
    tok = lambda w, dt: (pl.BlockSpec((1, tm, w), lambda b, i: (b, i, 0)), jax.ShapeDtypeStruct((bsz, seq, w), dt))
    outs = [tok(A_WIDTH, BF16), tok(A_WIDTH, F32), tok(A_WIDTH, F32),
            tok(B_WIDTH, BF16), tok(B_WIDTH, BF16), tok(B_WIDTH, BF16),
            tok(B_WIDTH, BF16), tok(D_MODEL, BF16), tok(D_MODEL, BF16), tok(SMALL_W, F32)]
    consts = [g1, wa, ba, wb, bb, wg, bg, wsh, wsl, bs, gq, gk, bd]
    return pl.pallas_call(
        _inproj_kernel,
        grid=(bsz, seq // tm),
        in_specs=[pl.BlockSpec((1, tm, D_MODEL), lambda b, i: (b, i, 0)),
                  pl.BlockSpec((1, 1, mod.shape[-1]), lambda b, i: (b, 0, 0))]
                 + [_resident(c.shape) for c in consts],
        out_specs=[o[0] for o in outs],
        out_shape=[o[1] for o in outs],
        compiler_params=pltpu.CompilerParams(dimension_semantics=("arbitrary", "arbitrary"),
                                             vmem_limit_bytes=VMEM_LIMIT),
        name="inproj",
    )(x, mod, *consts)


def _cumsum_kernel(x_ref, u_ref, o_ref, carry_ref):
    @pl.when(pl.program_id(1) == 0)
    def _():
        carry_ref[...] = jnp.zeros_like(carry_ref)
    tc = x_ref.shape[-1]
    f = _dot_f32(x_ref[0], u_ref[...]) + carry_ref[:, 0:1]
    o_ref[0] = f
    carry_ref[...] = jnp.broadcast_to(f[:, tc - 1:tc], carry_ref.shape)


def _cumsum_lanes(x, tc):
    bsz, rows, seq = x.shape
    upper = (lax.broadcasted_iota(jnp.int32, (tc, tc), 0) <= lax.broadcasted_iota(jnp.int32, (tc, tc), 1)).astype(F32)
    return pl.pallas_call(
        _cumsum_kernel,
        grid=(bsz, seq // tc),
        in_specs=[pl.BlockSpec((1, rows, tc), lambda b, i: (b, 0, i)), _resident((tc, tc))],
        out_specs=pl.BlockSpec((1, rows, tc), lambda b, i: (b, 0, i)),
        out_shape=jax.ShapeDtypeStruct((bsz, rows, seq), F32),
        scratch_shapes=[pltpu.VMEM((rows, LANES), F32)],
        compiler_params=pltpu.CompilerParams(dimension_semantics=("arbitrary", "arbitrary"),
                                             vmem_limit_bytes=VMEM_LIMIT),
        name="cumsum",
    )(x, upper)


def _fox_kernel(q_ref, k_ref, v_ref, f_ref, o_ref, m_ref, l_ref, acc_ref, *, tq, tk, qoff):
    qi = pl.program_id(2)
    ki = pl.program_id(3)
    q_lo = qoff + qi * tq
    last = (q_lo + tq - 1) // tk

    @pl.when(ki == 0)
    def _():
        m_ref[...] = jnp.full_like(m_ref, NEG)
        l_ref[...] = jnp.zeros_like(l_ref)
        acc_ref[...] = jnp.zeros_like(acc_ref)

    def step(masked):
        q = q_ref[0]
        k = k_ref[0].astype(BF16)
        v = v_ref[0].astype(BF16)
        lane = lax.broadcasted_iota(jnp.int32, (tq, LANES), 1)
        first_head = lane < A_HEAD_DIM
        alphas, pvs = [], []
        for h in range(2):
            qh = jnp.where(first_head if h == 0 else ~first_head, q, jnp.zeros_like(q))
            s = _dot_nt(qh, k) - f_ref[0, 0, h:h + 1, :]
            if masked:
                kpos = ki * tk + lax.broadcasted_iota(jnp.int32, (tq, tk), 1)
                qpos = q_lo + lax.broadcasted_iota(jnp.int32, (tq, tk), 0)
                s = jnp.where(kpos <= qpos, s, NEG)
            m_prev = m_ref[h]
            m_new = jnp.maximum(m_prev, jnp.max(s, axis=-1, keepdims=True))
            alpha = jnp.exp(m_prev - m_new)
            p = jnp.exp(s - m_new)
            l_ref[h] = alpha * l_ref[h] + jnp.sum(p, axis=-1, keepdims=True)
            m_ref[h] = m_new
            alphas.append(alpha)
            pvs.append(_dot(p.astype(BF16), v))
        acc_ref[...] = (acc_ref[...] * jnp.where(first_head, alphas[0], alphas[1])
                        + jnp.where(first_head, pvs[0], pvs[1]))

    needs_mask = (ki + 1) * tk - 1 > q_lo

    @pl.when((ki <= last) & needs_mask)
    def _():
        step(True)

    @pl.when((ki <= last) & jnp.logical_not(needs_mask))
    def _():
        step(False)

    @pl.when(ki == last)
    def _():
        lane = lax.broadcasted_iota(jnp.int32, (tq, LANES), 1)
        inv = jnp.where(lane < A_HEAD_DIM, 1.0 / l_ref[0], 1.0 / l_ref[1])
        o_ref[0] = (acc_ref[...] * inv).astype(o_ref.dtype)


def _fox_attention(q, k, v, fk, *, tq, tk, qoff):
    bsz, lq, _ = q.shape
    seq = k.shape[1]
    nq, nk = lq // tq, seq // tk
    pairs = A_HEADS // 2

    def kblock(qi, ki):
        return jnp.minimum(ki, (qoff + qi * tq + tq - 1) // tk)

    return pl.pallas_call(
        functools.partial(_fox_kernel, tq=tq, tk=tk, qoff=qoff),
        grid=(bsz, pairs, nq, nk),
        in_specs=[pl.BlockSpec((1, tq, LANES), lambda b, hp, qi, ki: (b, qi, hp)),
                  pl.BlockSpec((1, tk, LANES), lambda b, hp, qi, ki: (b, kblock(qi, ki), hp)),
                  pl.BlockSpec((1, tk, LANES), lambda b, hp, qi, ki: (b, kblock(qi, ki), hp)),
                  pl.BlockSpec((1, 1, 2, tk), lambda b, hp, qi, ki: (b, hp, 0, kblock(qi, ki)))],
        out_specs=pl.BlockSpec((1, tq, LANES), lambda b, hp, qi, ki: (b, qi, hp)),
        out_shape=jax.ShapeDtypeStruct((bsz, lq, A_WIDTH), BF16),
        scratch_shapes=[pltpu.VMEM((2, tq, 1), F32), pltpu.VMEM((2, tq, 1), F32), pltpu.VMEM((tq, LANES), F32)],
        compiler_params=pltpu.CompilerParams(
            dimension_semantics=("arbitrary", "arbitrary", "arbitrary", "arbitrary"), vmem_limit_bytes=VMEM_LIMIT),
        name="fox_attention",
    )(q, k, v, fk)


def _mlstm_kernel(q_ref, k_ref, v_ref, sm_ref, smt_ref, ltri_ref, utri_ref, c0_ref, n0_ref, m0_ref,
                  h_ref, c_ref, n_ref, m_ref, cs_ref, ns_ref, ms_ref, *, chunk):
    ci = pl.program_id(1)

    @pl.when(ci == 0)
    def _():
        cs_ref[...] = c0_ref[0]
        ns_ref[...] = n0_ref[0]
        ms_ref[...] = m0_ref[0]

    sm = sm_ref[0]
    smt = smt_ref[0]
    bcol_all = _dot_f32(ltri_ref[...], sm)
    brow_all = _dot_f32(smt, utri_ref[...])
    row = lax.broadcasted_iota(jnp.int32, (chunk, chunk), 0)
    colx = lax.broadcasted_iota(jnp.int32, (chunk, chunk), 1)
    causal = colx <= row
    i0 = A_HEADS
    f0 = A_HEADS + B_HEADS
    for h in range(B_HEADS):
        sl = slice(h * B_DIM, (h + 1) * B_DIM)
        qh, kh, vh = q_ref[0, :, sl], k_ref[0, :, sl], v_ref[0, :, sl]
        bcol = bcol_all[:, f0 + h:f0 + h + 1]
        brow = brow_all[f0 + h:f0 + h + 1, :]
        irow = smt[i0 + h:i0 + h + 1, :]
        icol = sm[:, i0 + h:i0 + h + 1]
        m0 = ms_ref[h][:, 0:1]
        dmat = jnp.where(causal, bcol - brow + irow, NEG)
        inter = m0 + bcol
        m = jnp.maximum(jnp.max(dmat, axis=-1, keepdims=True), inter)
        w = jnp.exp(dmat - m)
        a = jnp.exp(inter - m)
        ws = w * _dot_nt(qh, kh)
        c_prev = cs_ref[h]
        n_prev = ns_ref[h]
        num = _dot(ws.astype(BF16), vh) + a * _dot(qh, c_prev.astype(BF16))
        den = jnp.sum(ws, axis=-1, keepdims=True) + a * jnp.sum(qh.astype(F32) * n_prev, axis=-1, keepdims=True)
        hh = num / jnp.maximum(jnp.abs(den), jnp.exp(-m))
        hn = hh * lax.rsqrt(jnp.mean(hh * hh, axis=-1, keepdims=True) + EPS)
        h_ref[0, :, sl] = hn.astype(h_ref.dtype)
        m_last = m[chunk - 1:chunk, :]
        b_last = bcol[chunk - 1:chunk, :]
        w_last = jnp.exp(b_last - bcol + icol - m_last)
        a_last = jnp.exp(m0 + b_last - m_last)
        kw = kh.astype(F32) * w_last
        cs_ref[h] = a_last * c_prev + _dot_tn(kw.astype(BF16), vh)
        ns_ref[h] = a_last * n_prev + jnp.sum(kw, axis=0, keepdims=True)
        ms_ref[h] = jnp.broadcast_to(m_last, (1, LANES))

    @pl.when(ci == pl.num_programs(1) - 1)
    def _():
        c_ref[0] = cs_ref[...]
        n_ref[0] = ns_ref[...]
        m_ref[0] = ms_ref[...]


def _mlstm(q, k, v, small, small_t, c0, n0, m0, *, chunk):
    bsz, seq, _ = q.shape
    nc = seq // chunk
    r = lax.broadcasted_iota(jnp.int32, (chunk, chunk), 0)
    c = lax.broadcasted_iota(jnp.int32, (chunk, chunk), 1)
    ltri = (c <= r).astype(F32)
    utri = (r <= c).astype(F32)
    tokb = pl.BlockSpec((1, chunk, B_WIDTH), lambda b, i: (b, i, 0))
    st = lambda shp: pl.BlockSpec((1,) + shp, lambda b, i: (b,) + (0,) * len(shp))
    return pl.pallas_call(
        functools.partial(_mlstm_kernel, chunk=chunk),
        grid=(bsz, nc),
        in_specs=[tokb, tokb, tokb,
                  pl.BlockSpec((1, chunk, SMALL_W), lambda b, i: (b, i, 0)),
                  pl.BlockSpec((1, 16, chunk), lambda b, i: (b, 0, i)),
                  _resident((chunk, chunk)), _resident((chunk, chunk)),
                  st((B_HEADS, B_DIM, B_DIM)), st((B_HEADS, 1, B_DIM)), st((B_HEADS, 1, LANES))],
        out_specs=[tokb, st((B_HEADS, B_DIM, B_DIM)), st((B_HEADS, 1, B_DIM)), st((B_HEADS, 1, LANES))],
        out_shape=[jax.ShapeDtypeStruct((bsz, seq, B_WIDTH), BF16),
                   jax.ShapeDtypeStruct((bsz, B_HEADS, B_DIM, B_DIM), F32),
                   jax.ShapeDtypeStruct((bsz, B_HEADS, 1, B_DIM), F32),
                   jax.ShapeDtypeStruct((bsz, B_HEADS, 1, LANES), F32)],
        scratch_shapes=[pltpu.VMEM((B_HEADS, B_DIM, B_DIM), F32), pltpu.VMEM((B_HEADS, 1, B_DIM), F32),
                        pltpu.VMEM((B_HEADS, 1, LANES), F32)],
        compiler_params=pltpu.CompilerParams(dimension_semantics=("arbitrary", "arbitrary"),
                                             vmem_limit_bytes=VMEM_LIMIT),
        name="mlstm",
    )(q, k, v, small, small_t, ltri, utri, c0, n0, m0)


def _postmix_kernel(ha_ref, hb_ref, ob_ref, sga_ref, sgb_ref, x_ref, mod_ref, ghb_ref, g2_ref,
                    wpa_ref, wpb_ref, wo_ref, wpq_ref, x1_ref, h2_ref, pq_ref):
    mod = mod_ref[0]
    gt1 = mod[:, 2 * D_MODEL:3 * D_MODEL]
    sh2 = mod[:, 3 * D_MODEL:4 * D_MODEL]
    sc2 = mod[:, 4 * D_MODEL:5 * D_MODEL]
    hb = (hb_ref[0].astype(F32) * ghb_ref[...] * ob_ref[0].astype(F32)).astype(BF16)
    merged = (sga_ref[0].astype(F32) * _dot(ha_ref[0], wpa_ref[...])
              + sgb_ref[0].astype(F32) * _dot(hb, wpb_ref[...]))
    x1 = x_ref[0] + gt1 * _dot(merged.astype(BF16), wo_ref[...])
    x1_ref[0] = x1
    h2 = x1 * lax.rsqrt(jnp.mean(x1 * x1, axis=-1, keepdims=True) + EPS)
    h2 = h2 * g2_ref[...] * (1.0 + sc2) + sh2
    h2_ref[0] = h2
    pq_ref[0] = _dot(h2.astype(BF16), wpq_ref[...]).astype(BF16)


def _postmix(ha, hb, ob, sga, sgb, x, mod, ghb, g2, wpa, wpb, wo, wpq, tm):
    bsz, seq, _ = x.shape
    tok = lambda w: pl.BlockSpec((1, tm, w), lambda b, i: (b, i, 0))
    consts = [ghb, g2, wpa, wpb, wo, wpq]
    return pl.pallas_call(
        _postmix_kernel,
        grid=(bsz, seq // tm),
        in_specs=[tok(A_WIDTH), tok(B_WIDTH), tok(B_WIDTH), tok(D_MODEL), tok(D_MODEL), tok(D_MODEL),
                  pl.BlockSpec((1, 1, mod.shape[-1]), lambda b, i: (b, 0, 0))]
                 + [_resident(c.shape) for c in consts],
        out_specs=[tok(D_MODEL), tok(D_MODEL), tok(D_MODEL)],
        out_shape=[jax.ShapeDtypeStruct((bsz, seq, D_MODEL), F32), jax.ShapeDtypeStruct((bsz, seq, D_MODEL), F32),
                   jax.ShapeDtypeStruct((bsz, seq, D_MODEL), BF16)],
        compiler_params=pltpu.CompilerParams(dimension_semantics=("arbitrary", "arbitrary"),
                                             vmem_limit_bytes=VMEM_LIMIT),
        name="postmix",
    )(ha, hb, ob, sga, sgb, x, mod, *consts)


def _top16(x):
    n = x.shape[0]
    rows = lax.broadcasted_iota(jnp.int32, x.shape, 0)
    vals, idxs = [], []
    for _ in range(P_TOPK):
        mx = jnp.max(x, axis=0, keepdims=True)
        am = jnp.min(jnp.where(x == mx, rows, n), axis=0, keepdims=True)
        vals.append(mx)
        idxs.append(am)
        x = jnp.where(rows == am, -jnp.inf, x)
    return jnp.concatenate(vals, axis=0), jnp.concatenate(idxs, axis=0)


def _pick(table, sel):
    out = jnp.zeros_like(table)
    for a in range(P_TOPK):
        out = jnp.where(sel == a, table[a:a + 1, :], out)
    return out


def _topk_kernel(pq_ref, sk_ref, idx_ref, g_ref):
    pq = pq_ref[...]
    for h in range(P_HEADS):
        sv, si = [], []
        for p in range(2):
            j = 2 * h + p
            s = _dot_nt(sk_ref[j], pq[:, j * P_HALF:(j + 1) * P_HALF])
            v, i = _top16(s)
            sv.append(v)
            si.append(i)
        cand = jnp.concatenate([sv[0][a:a + 1, :] + sv[1] for a in range(P_TOPK)], axis=0)
        fv, fi = _top16(cand)
        e = _pick(si[0], fi >> 4) * P_KEYS + _pick(si[1], fi & (P_TOPK - 1))
        ex = jnp.exp(fv - fv[0:1, :])
        g = ex / jnp.sum(ex, axis=0, keepdims=True)
        idx_ref[h * P_TOPK:(h + 1) * P_TOPK, :] = e
        g_ref[h * P_TOPK:(h + 1) * P_TOPK, :] = g


def _peer_topk(pq, sub_keys, tt):
    t = pq.shape[0]
    return pl.pallas_call(
        _topk_kernel,
        grid=(t // tt,),
        in_specs=[pl.BlockSpec((tt, D_MODEL), lambda i: (i, 0)), _resident(sub_keys.shape)],
        out_specs=[pl.BlockSpec((P_SEL, tt), lambda i: (0, i)), pl.BlockSpec((P_SEL, tt), lambda i: (0, i))],
        out_shape=[jax.ShapeDtypeStruct((P_SEL, t), jnp.int32), jax.ShapeDtypeStruct((P_SEL, t), F32)],
        compiler_params=pltpu.CompilerParams(dimension_semantics=("arbitrary",), vmem_limit_bytes=VMEM_LIMIT),
        name="peer_topk",
    )(pq, sub_keys)


def _pack_table(t):
    e, d = t.shape
    bits = lax.bitcast_convert_type(t.astype(BF16), jnp.uint16).astype(jnp.uint32)
    bits = bits.reshape(e, 2, d // (2 * LANES), LANES)
    return bits[:, 0] | (bits[:, 1] << 16)


def _halves(word):
    lo = pltpu.bitcast(word << 16, F32)
    hi = pltpu.bitcast(word & jnp.uint32(0xFFFF0000), F32)
    return lo, hi


def _merge_sublanes(a, b, step):
    sub = lax.broadcasted_iota(jnp.int32, (SUBLANES, LANES), 0)
    low = (sub & step) == 0
    return jnp.where(low, a + pltpu.roll(a, SUBLANES - step, axis=0), b + pltpu.roll(b, step, axis=0))


def _fold8(parts):
    pair = lambda i, j: jnp.concatenate([parts[i], parts[j]], axis=0)
    even = _merge_sublanes(pair(0, 4), pair(2, 6), 2)
    odd = _merge_sublanes(pair(1, 5), pair(3, 7), 2)
    return _merge_sublanes(even, odd, 1)


def _peer_act_kernel(e_ref, x_ref, g_ref, tab_ref, w_ref, q_ref, *, tb):
    def tok(t, carry):
        x = x_ref[t]
        xlo, xhi = x[0:4], x[4:8]
        for grp in range(P_SEL // SUBLANES):
            parts = []
            for j in range(SUBLANES):
                lo, hi = _halves(tab_ref[e_ref[t, grp * SUBLANES + j]])
                parts.append(lo * xlo + hi * xhi)
            q_ref[t, grp * SUBLANES:(grp + 1) * SUBLANES, :] = _fold8(parts)
        return carry
    lax.fori_loop(0, tb, tok, 0)

    ones = jnp.ones((LANES, LANES), BF16)
    lane = lax.broadcasted_iota(jnp.int32, (P_SEL, LANES), 1)
    act = jnp.zeros((P_SEL, tb), F32)
    for t in range(tb):
        qh, ql = _split2(q_ref[t])
        r = _dot(qh, ones) + _dot(ql, ones)
        act = jnp.where(lane == t, r, act)
    gelu = 0.5 * act * (1.0 + lax.erf(act * (2.0 ** -0.5)))
    w_ref[...] = g_ref[...] * gelu


def _peer_act(e_tok, x3, g_t, table, tb):
    t = x3.shape[0]
    return pl.pallas_call(
        functools.partial(_peer_act_kernel, tb=tb),
        grid=(t // tb,),
        in_specs=[pl.BlockSpec((tb, P_SEL), lambda i: (i, 0), memory_space=pltpu.SMEM),
                  pl.BlockSpec((tb, SUBLANES, LANES), lambda i: (i, 0, 0)),
                  pl.BlockSpec((P_SEL, tb), lambda i: (0, i)),
                  _resident(table.shape)],
        out_specs=pl.BlockSpec((P_SEL, tb), lambda i: (0, i)),
        out_shape=jax.ShapeDtypeStruct((P_SEL, t), F32),
        scratch_shapes=[pltpu.VMEM((tb, P_SEL, LANES), F32)],
        compiler_params=pltpu.CompilerParams(dimension_semantics=("arbitrary",),
                                             vmem_limit_bytes=TABLE_VMEM_LIMIT),
        name="peer_act",
    )(e_tok, x3, g_t, table)


def _peer_out_kernel(e_ref, w_ref, x1_ref, gt_ref, tab_ref, o_ref, wb_ref, *, tb, group):
    wh, wl = _split2(w_ref[...])
    rows = lax.broadcasted_iota(jnp.int32, (tb, LANES), 0)
    for t in range(tb):
        onehot = jnp.where(rows == t, 1.0, 0.0).astype(BF16)
        wb_ref[t] = _dot(wh, onehot) + _dot(wl, onehot)

    def tok(t, carry):
        lo_acc = [jnp.zeros((4, LANES), F32), jnp.zeros((4, LANES), F32)]
        hi_acc = [jnp.zeros((4, LANES), F32), jnp.zeros((4, LANES), F32)]
        for c in range(P_SEL):
            lo, hi = _halves(tab_ref[e_ref[t, c]])
            w = jnp.broadcast_to(wb_ref[t, c:c + 1, :], (4, LANES))
            lo_acc[c % 2] = lo_acc[c % 2] + lo * w
            hi_acc[c % 2] = hi_acc[c % 2] + hi * w
        y = jnp.concatenate([lo_acc[0] + lo_acc[1], hi_acc[0] + hi_acc[1]], axis=0)
        o_ref[t] = x1_ref[t] + gt_ref[t // group] * y
        return carry
    lax.fori_loop(0, tb, tok, 0)


def _peer_out(e_tok, w_t, x1_3, gt_groups, table, tb, group):
    t = x1_3.shape[0]
    return pl.pallas_call(
        functools.partial(_peer_out_kernel, tb=tb, group=group),
        grid=(t // tb,),
        in_specs=[pl.BlockSpec((tb, P_SEL), lambda i: (i, 0), memory_space=pltpu.SMEM),
                  pl.BlockSpec((P_SEL, tb), lambda i: (0, i)),
                  pl.BlockSpec((tb, SUBLANES, LANES), lambda i: (i, 0, 0)),
                  pl.BlockSpec((tb // group, SUBLANES, LANES), lambda i: (i, 0, 0)),
                  _resident(table.shape)],
        out_specs=pl.BlockSpec((tb, SUBLANES, LANES), lambda i: (i, 0, 0)),
        out_shape=jax.ShapeDtypeStruct((t, SUBLANES, LANES), F32),
        scratch_shapes=[pltpu.VMEM((tb, P_SEL, LANES), F32)],
        compiler_params=pltpu.CompilerParams(dimension_semantics=("arbitrary",),
                                             vmem_limit_bytes=TABLE_VMEM_LIMIT),
        name="peer_out",
    )(e_tok, w_t, x1_3, gt_groups, table)


def _prep_weights(w_in, b_in, g_qa, g_ka):
    sizes = (A_WIDTH, A_WIDTH, A_WIDTH, A_HEADS, B_WIDTH, B_WIDTH, B_WIDTH, B_HEADS, B_HEADS, B_WIDTH,
             D_MODEL, D_MODEL)
    offs = [0]
    for s in sizes:
        offs.append(offs[-1] + s)
    col = lambda a, i: a[..., offs[i]:offs[i + 1]]
    cat = lambda a, ids: jnp.concatenate([col(a, i) for i in ids], axis=-1)
    wa, ba = cat(w_in, (0, 1, 2)), cat(b_in, (0, 1, 2))
    wb, bb = cat(w_in, (4, 5, 6)), cat(b_in, (4, 5, 6))
    wg, bg = cat(w_in, (9, 10, 11)), cat(b_in, (9, 10, 11))
    ws, bs = cat(w_in, (3, 7, 8)), cat(b_in, (3, 7, 8))
    pad = SMALL_W - ws.shape[-1]
    ws = jnp.pad(ws, ((0, 0), (0, pad)))
    bs = jnp.pad(bs, ((0, pad),))
    wsh, wsl = _split2(ws)
    blk = lax.broadcasted_iota(jnp.int32, (A_WIDTH, A_WIDTH), 0) // A_HEAD_DIM
    blk_t = lax.broadcasted_iota(jnp.int32, (A_WIDTH, A_WIDTH), 1) // A_HEAD_DIM
    bd = jnp.where(blk == blk_t, 1.0 / A_HEAD_DIM, 0.0).astype(BF16)
    r = lambda v: v.reshape(1, -1)
    return (wa.astype(BF16), r(ba), wb.astype(BF16), r(bb), wg.astype(BF16), r(bg), wsh, wsl, r(bs),
            r(g_qa), r(g_ka), bd)


def _gate_rows(small):
    return jnp.swapaxes(small[..., 0:16], 1, 2)


def kernel(x_prompt, x_sample, c_prompt, c_sample, cache_k, cache_v, cache_logf, state_C, state_n, state_m,
           w_mod, b_mod, g_norm1, g_norm2, w_in, b_in, g_qa, g_ka, g_hb, w_pa, w_pb, w_o, w_pq, sub_keys,
           expert_u, expert_v):
    assert w_mod.shape[0] == 1, "single-layer step"
    bp, sp, _ = x_prompt.shape
    bs, ss, _ = x_sample.shape
    past = cache_k.shape[2]
    inw = _prep_weights(w_in[0], b_in[0], g_qa[0], g_ka[0])
    g1, g2 = g_norm1[0].reshape(1, -1), g_norm2[0].reshape(1, -1)
    ghb = g_hb[0].reshape(1, -1)
    post_w = (w_pa[0].astype(BF16), w_pb[0].astype(BF16), w_o[0].astype(BF16), w_pq[0].astype(BF16))
    skeys = sub_keys[0].reshape(2 * P_HEADS, P_KEYS, P_HALF).astype(BF16)
    tab_u, tab_v = _pack_table(expert_u[0]), _pack_table(expert_v[0])

    mod_p = _modulation(c_prompt, w_mod[0], b_mod[0])[:, None, :]
    mod_s = _modulation(c_sample, w_mod[0], b_mod[0])[:, None, :]

    tm = 256
    (qa, ka, va, qb, kb, vb, ob, sga, sgb, small) = _inproj(x_prompt, mod_p, g1, inw, tm)
    srow = _gate_rows(small)
    fk = _cumsum_lanes(srow[:, 0:A_HEADS, :], 512).reshape(bp, A_HEADS // 2, 2, sp)
    ha = _fox_attention(qa, ka, va, fk, tq=256, tk=512, qoff=0)
    zc = jnp.zeros((bp, B_HEADS, B_DIM, B_DIM), F32)
    zn = jnp.zeros((bp, B_HEADS, 1, B_DIM), F32)
    hb, c_p, n_p, m_p = _mlstm(qb, kb, vb, small, srow, zc, zn, zn, chunk=128)
    x1_p, h2_p, pq_p = _postmix(ha, hb, ob, sga, sgb, x_prompt, mod_p, ghb, g2, *post_w, tm)
    state_p = (ka.reshape(1, bp, sp, A_HEADS, A_HEAD_DIM), va.reshape(1, bp, sp, A_HEADS, A_HEAD_DIM),
               small[None, :, :, 0:A_HEADS], c_p[None], n_p.reshape(1, bp, B_HEADS, B_DIM),
               m_p[None, :, :, 0, 0])

    (qa, ka, va, qb, kb, vb, ob, sga, sgb, small) = _inproj(x_sample, mod_s, g1, inw, ss)
    tk = 512
    s_all = past + ss
    s_pad = -(-s_all // tk) * tk
    padk = lambda a: jnp.pad(a, ((0, 0), (0, s_pad - s_all), (0, 0)))
    kk = padk(jnp.concatenate([cache_k[0].reshape(bs, past, A_WIDTH), ka], axis=1))
    vv = padk(jnp.concatenate([cache_v[0].reshape(bs, past, A_WIDTH), va], axis=1))
    lf_all = padk(jnp.concatenate([cache_logf[0], small[:, :, 0:A_HEADS]], axis=1))
    fk = _cumsum_lanes(jnp.swapaxes(lf_all, 1, 2), tk).reshape(bs, A_HEADS // 2, 2, s_pad)
    ha = _fox_attention(qa, kk, vv, fk, tq=ss, tk=tk, qoff=past)
    chunk = 128
    padt = lambda a: jnp.pad(a, ((0, 0), (0, chunk - ss), (0, 0)))
    colid = jnp.arange(SMALL_W)
    is_input = (colid >= A_HEADS) & (colid < A_HEADS + B_HEADS)
    real = (jnp.arange(chunk) < ss)[None, :, None]
    small_pad = jnp.where(real, padt(small), jnp.where(is_input, NEG, 0.0)[None, None, :])
    c0 = state_C[0].astype(F32)
    n0 = state_n[0].astype(F32).reshape(bs, B_HEADS, 1, B_DIM)
    m0 = jnp.broadcast_to(state_m[0].astype(F32)[:, :, None, None], (bs, B_HEADS, 1, LANES))
    hb, c_s, n_s, m_s = _mlstm(padt(qb), padt(kb), padt(vb), small_pad, _gate_rows(small_pad), c0, n0, m0,
                               chunk=chunk)
    x1_s, h2_s, pq_s = _postmix(ha, hb[:, 0:ss], ob, sga, sgb, x_sample, mod_s, ghb, g2, *post_w, ss)
    state_s = (ka.reshape(1, bs, ss, A_HEADS, A_HEAD_DIM), va.reshape(1, bs, ss, A_HEADS, A_HEAD_DIM),
               small[None, :, :, 0:A_HEADS], c_s[None], n_s.reshape(1, bs, B_HEADS, B_DIM),
               m_s[None, :, :, 0, 0])

    tp, ts = bp * sp, bs * ss
    tt = 256
    assert tp % tt == 0 and ts % min(tt, ts) == 0
    idx_p, g_p = _peer_topk(pq_p.reshape(tp, D_MODEL), skeys, tt)
    idx_s, g_s = _peer_topk(pq_s.reshape(ts, D_MODEL), skeys, min(tt, ts))
    idx_t = jnp.concatenate([idx_p, idx_s], axis=1)
    g_t = jnp.concatenate([g_p, g_s], axis=1)
    e_tok = jnp.swapaxes(idx_t, 0, 1)
    tok3 = lambda a, b: jnp.concatenate([a.reshape(tp, SUBLANES, LANES), b.reshape(ts, SUBLANES, LANES)], axis=0)
    x3 = tok3(h2_p, h2_s)
    x1_3 = tok3(x1_p, x1_s)
    tb = LANES
    assert (tp + ts) % tb == 0 and sp % ss == 0 and tb % ss == 0
    w_t = _peer_act(e_tok, x3, g_t, tab_u, tb)
    group = ss
    gt2 = lambda mod, rep: jnp.repeat(mod[:, 0, 5 * D_MODEL:6 * D_MODEL], rep, axis=0)
    gt_groups = jnp.concatenate([gt2(mod_p, sp // group), gt2(mod_s, 1)], axis=0).reshape(-1, SUBLANES, LANES)
    y3 = _peer_out(e_tok, w_t, x1_3, gt_groups, tab_v, tb, group)
    y_p = y3[0:tp].reshape(bp, sp, D_MODEL)
    y_s = y3[tp:].reshape(bs, ss, D_MODEL)
    return (y_p, y_s) + state_p + state_s
```

```python
import functools

import jax
import jax.numpy as jnp
from jax import lax
from jax.experimental import pallas as pl
from jax.experimental.pallas import tpu as pltpu

EPS = 1e-6
NEG = -1e30
D_MODEL = 1024
A_HEADS = 8
A_HEAD_DIM = 64
A_WIDTH = A_HEADS * A_HEAD_DIM
B_HEADS = 4
B_DIM = 128
B_WIDTH = B_HEADS * B_DIM
P_HEADS = 8
P_KEYS = 128
P_HALF = 64
P_TOPK = 16
P_SEL = P_HEADS * P_TOPK
LANES = 128
SUBLANES = 8
SMALL_W = 128
VMEM_LIMIT = 48 * 2**20
TABLE_VMEM_LIMIT = 56 * 2**20
F32 = jnp.float32
BF16 = jnp.bfloat16


def _dot(a, b):
    return jnp.dot(a, b, preferred_element_type=F32)


def _dot_nt(a, b):
    return lax.dot_general(a, b, (((1,), (1,)), ((), ())), preferred_element_type=F32)


def _dot_tn(a, b):
    return lax.dot_general(a, b, (((0,), (0,)), ((), ())), preferred_element_type=F32)


def _dot_f32(a, b):
    return jnp.dot(a, b, preferred_element_type=F32, precision=lax.Precision.HIGHEST)


def _split2(x):
    hi = x.astype(BF16)
    lo = (x - hi.astype(F32)).astype(BF16)
    return hi, lo


def _log_sigmoid(x):
    return jnp.minimum(x, 0.0) - jnp.log(1.0 + jnp.exp(-jnp.abs(x)))


def _sigmoid(x):
    return 1.0 / (1.0 + jnp.exp(-x))


def _resident(shape):
    nd = len(shape)
    return pl.BlockSpec(shape, lambda *_: (0,) * nd, pipeline_mode=pl.Buffered(1))


def _mod_kernel(c_ref, w_ref, b_ref, o_ref):
    c = c_ref[...]
    s = c * _sigmoid(c)
    o_ref[...] = _dot_f32(s, w_ref[...]) + b_ref[...]


def _modulation(c, w_mod, b_mod):
    bsz = c.shape[0]
    n = w_mod.shape[1]
    tn = 1024
    return pl.pallas_call(
        _mod_kernel,
        grid=(n // tn,),
        in_specs=[pl.BlockSpec((bsz, D_MODEL), lambda j: (0, 0)),
                  pl.BlockSpec((D_MODEL, tn), lambda j: (0, j)),
                  pl.BlockSpec((1, tn), lambda j: (0, j))],
        out_specs=pl.BlockSpec((bsz, tn), lambda j: (0, j)),
        out_shape=jax.ShapeDtypeStruct((bsz, n), F32),
        compiler_params=pltpu.CompilerParams(dimension_semantics=("arbitrary",), vmem_limit_bytes=VMEM_LIMIT),
        name="modulation",
    )(c, w_mod, b_mod.reshape(1, n))


def _inproj_kernel(x_ref, mod_ref, g1_ref, wa_ref, ba_ref, wb_ref, bb_ref, wg_ref, bg_ref,
                   wsh_ref, wsl_ref, bs_ref, gq_ref, gk_ref, bd_ref,
                   qa_ref, ka_ref, va_ref, qb_ref, kb_ref, vb_ref, ob_ref, sga_ref, sgb_ref, small_ref):
    x = x_ref[0]
    mod = mod_ref[0]
    sh1 = mod[:, 0:D_MODEL]
    sc1 = mod[:, D_MODEL:2 * D_MODEL]
    h = x * lax.rsqrt(jnp.mean(x * x, axis=-1, keepdims=True) + EPS)
    h = h * g1_ref[...] * (1.0 + sc1) + sh1
    hh, hl = _split2(h)

    za = _dot(hh, wa_ref[...]) + ba_ref[...]
    bd = bd_ref[...]

    def head_norm(z):
        msq = _dot((z * z).astype(BF16), bd)
        return z * lax.rsqrt(msq + EPS)

    q = head_norm(za[:, 0:A_WIDTH]) * gq_ref[...]
    k = head_norm(za[:, A_WIDTH:2 * A_WIDTH]) * gk_ref[...]
    qa_ref[0] = (q * (LOG2E * A_HEAD_DIM ** -0.5)).astype(BF16)
    ka_ref[0] = k
    va_ref[0] = za[:, 2 * A_WIDTH:3 * A_WIDTH]

    zb = _dot(hh, wb_ref[...]) + bb_ref[...]
    qb_ref[0] = zb[:, 0:B_WIDTH].astype(BF16)
    kb_ref[0] = (zb[:, B_WIDTH:2 * B_WIDTH] * (B_DIM ** -0.5)).astype(BF16)
    vb_ref[0] = zb[:, 2 * B_WIDTH:3 * B_WIDTH].astype(BF16)

    zg = _sigmoid(_dot(hh, wg_ref[...]) + bg_ref[...])
    ob_ref[0] = zg[:, 0:B_WIDTH].astype(BF16)
    sga_ref[0] = zg[:, B_WIDTH:B_WIDTH + D_MODEL].astype(BF16)
    sgb_ref[0] = zg[:, B_WIDTH + D_MODEL:B_WIDTH + 2 * D_MODEL].astype(BF16)

    zs = _dot(hh, wsh_ref[...]) + _dot(hl, wsh_ref[...]) + _dot(hh, wsl_ref[...]) + bs_ref[...]
    col = lax.broadcasted_iota(jnp.int32, zs.shape, 1)
    is_input_gate = (col >= A_HEADS) & (col < A_HEADS + B_HEADS)
    small_ref[0] = jnp.where(is_input_gate, zs, _log_sigmoid(zs))


def _inproj(x, mod, g1, wts, tm):
    bsz, seq, _ = x.shape
    (wa, ba, wb, bb, wg, bg, wsh, wsl, bs, gq, gk, bd) = wts
    tok = lambda w, dt: (pl.BlockSpec((1, tm, w), lambda b, i: (b, i, 0)), jax.ShapeDtypeStruct((bsz, seq, w), dt))
    outs = [tok(A_WIDTH, BF16), tok(A_WIDTH, F32), tok(A_WIDTH, F32),
            tok(B_WIDTH, BF16), tok(B_WIDTH, BF16), tok(B_WIDTH, BF16),
            tok(B_WIDTH, BF16), tok(D_MODEL, BF16), tok(D_MODEL, BF16), tok(SMALL_W, F32)]
    consts = [g1, wa, ba, wb, bb, wg, bg, wsh, wsl, bs, gq, gk, bd]
    return pl.pallas_call(
        _inproj_kernel,
        grid=(bsz, seq // tm),
        in_specs=[pl.BlockSpec((1, tm, D_MODEL), lambda b, i: (b, i, 0)),
                  pl.BlockSpec((1, 1, mod.shape[-1]), lambda b, i: (b, 0, 0))]
                 + [_resident(c.shape) for c in consts],
        out_specs=[o[0] for o in outs],
        out_shape=[o[1] for o in outs],
        compiler_params=pltpu.CompilerParams(dimension_semantics=("arbitrary", "arbitrary"),
                                             vmem_limit_bytes=VMEM_LIMIT),
        name="inproj",
    )(x, mod, *consts)


def _cumsum_kernel(x_ref, u_ref, o_ref, carry_ref):
    @pl.when(pl.program_id(1) == 0)
    def _():
        carry_ref[...] = jnp.zeros_like(carry_ref)
    tc = x_ref.shape[-1]
    f = _dot_f32(x_ref[0], u_ref[...]) + carry_ref[:, 0:1]
    o_ref[0] = f
    carry_ref[...] = jnp.broadcast_to(f[:, tc - 1:tc], carry_ref.shape)


def _cumsum_lanes(x, tc):
    bsz, rows, seq = x.shape
    upper = (lax.broadcasted_iota(jnp.int32, (tc, tc), 0) <= lax.broadcasted_iota(jnp.int32, (tc, tc), 1)).astype(F32)
    return pl.pallas_call(
        _cumsum_kernel,
        grid=(bsz, seq // tc),
        in_specs=[pl.BlockSpec((1, rows, tc), lambda b, i: (b, 0, i)), _resident((tc, tc))],
        out_specs=pl.BlockSpec((1, rows, tc), lambda b, i: (b, 0, i)),
        out_shape=jax.ShapeDtypeStruct((bsz, rows, seq), F32),
        scratch_shapes=[pltpu.VMEM((rows, LANES), F32)],
        compiler_params=pltpu.CompilerParams(dimension_semantics=("arbitrary", "arbitrary"),
                                             vmem_limit_bytes=VMEM_LIMIT),
        name="cumsum",
    )(x, upper)


LOG2E = 1.4426950408889634


def _fox_kernel(qt_ref, kt_ref, q_ref, k_ref, v_ref, f_ref, o_ref, m_ref, l_ref, acc_ref, *, tq, tk, qoff):
    step_id = pl.program_id(2)
    qi = qt_ref[step_id]
    ki = kt_ref[step_id]
    q_lo = qoff + qi * tq
    last = (q_lo + tq - 1) // tk

    @pl.when(ki == 0)
    def _():
        m_ref[...] = jnp.full_like(m_ref, NEG)
        l_ref[...] = jnp.zeros_like(l_ref)
        acc_ref[...] = jnp.zeros_like(acc_ref)

    def step(masked):
        q = q_ref[0]
        k = k_ref[0].astype(BF16)
        v = v_ref[0].astype(BF16)
        lane = lax.broadcasted_iota(jnp.int32, (tq, LANES), 1)
        first_head = lane < A_HEAD_DIM
        alphas, pvs = [], []
        for h in range(2):
            qh = jnp.where(first_head if h == 0 else ~first_head, q, jnp.zeros_like(q))
            s = _dot_nt(qh, k) - f_ref[0, 0, h:h + 1, :] * LOG2E
            if masked:
                kpos = ki * tk + lax.broadcasted_iota(jnp.int32, (tq, tk), 1)
                qpos = q_lo + lax.broadcasted_iota(jnp.int32, (tq, tk), 0)
                s = jnp.where(kpos <= qpos, s, NEG)
            m_prev = m_ref[h]
            m_new = jnp.maximum(m_prev, jnp.max(s, axis=-1, keepdims=True))
            alpha = jnp.exp2(m_prev - m_new)
            p = jnp.exp2(s - pltpu.repeat(m_new, tk // LANES, axis=1))
            l_ref[h] = alpha * l_ref[h] + jnp.sum(p, axis=-1, keepdims=True)
            m_ref[h] = m_new
            alphas.append(alpha)
            pvs.append(_dot(p.astype(BF16), v))
        acc_ref[...] = (acc_ref[...] * jnp.where(first_head, alphas[0], alphas[1])
                        + jnp.where(first_head, pvs[0], pvs[1]))

    needs_mask = (ki + 1) * tk - 1 > q_lo

    @pl.when(needs_mask)
    def _():
        step(True)

    @pl.when(jnp.logical_not(needs_mask))
    def _():
        step(False)

    @pl.when(ki == last)
    def _():
        lane = lax.broadcasted_iota(jnp.int32, (tq, LANES), 1)
        inv = jnp.where(lane < A_HEAD_DIM, 1.0 / l_ref[0], 1.0 / l_ref[1])
        o_ref[0] = (acc_ref[...] * inv).astype(o_ref.dtype)


def _fox_attention(q, k, v, fk, *, tq, tk, qoff):
    bsz, lq, _ = q.shape
    nq = lq // tq
    pairs = A_HEADS // 2
    sched = [(qi, ki) for qi in range(nq) for ki in range((qoff + qi * tq + tq - 1) // tk + 1)]
    qt = jnp.asarray([s[0] for s in sched], jnp.int32)
    kt = jnp.asarray([s[1] for s in sched], jnp.int32)
    grid_spec = pltpu.PrefetchScalarGridSpec(
        num_scalar_prefetch=2,
        grid=(bsz, pairs, len(sched)),
        in_specs=[pl.BlockSpec((1, tq, LANES), lambda b, hp, s, qt, kt: (b, qt[s], hp)),
                  pl.BlockSpec((1, tk, LANES), lambda b, hp, s, qt, kt: (b, kt[s], hp)),
                  pl.BlockSpec((1, tk, LANES), lambda b, hp, s, qt, kt: (b, kt[s], hp)),
                  pl.BlockSpec((1, 1, 2, tk), lambda b, hp, s, qt, kt: (b, hp, 0, kt[s]))],
        out_specs=pl.BlockSpec((1, tq, LANES), lambda b, hp, s, qt, kt: (b, qt[s], hp)),
        scratch_shapes=[pltpu.VMEM((2, tq, LANES), F32), pltpu.VMEM((2, tq, LANES), F32),
                        pltpu.VMEM((tq, LANES), F32)])
    return pl.pallas_call(
        functools.partial(_fox_kernel, tq=tq, tk=tk, qoff=qoff),
        grid_spec=grid_spec,
        out_shape=jax.ShapeDtypeStruct((bsz, lq, A_WIDTH), BF16),
        compiler_params=pltpu.CompilerParams(
            dimension_semantics=("arbitrary", "arbitrary", "arbitrary"), vmem_limit_bytes=VMEM_LIMIT),
        name="fox_attention",
    )(qt, kt, q, k, v, fk)


def _mlstm_kernel(q_ref, k_ref, v_ref, sm_ref, smt_ref, ltri_ref, utri_ref, c0_ref, n0_ref, m0_ref,
                  h_ref, c_ref, n_ref, m_ref, cs_ref, ns_ref, ms_ref, *, chunk):
    ci = pl.program_id(1)

    @pl.when(ci == 0)
    def _():
        cs_ref[...] = c0_ref[0]
        ns_ref[...] = n0_ref[0]
        ms_ref[...] = m0_ref[0]

    sm = sm_ref[0]
    smt = smt_ref[0]
    bcol_all = _dot_f32(ltri_ref[...], sm)
    brow_all = _dot_f32(smt, utri_ref[...])
    row = lax.broadcasted_iota(jnp.int32, (chunk, chunk), 0)
    colx = lax.broadcasted_iota(jnp.int32, (chunk, chunk), 1)
    causal = colx <= row
    i0 = A_HEADS
    f0 = A_HEADS + B_HEADS
    for h in range(B_HEADS):
        sl = slice(h * B_DIM, (h + 1) * B_DIM)
        qh, kh, vh = q_ref[0, :, sl], k_ref[0, :, sl], v_ref[0, :, sl]
        bcol = bcol_all[:, f0 + h:f0 + h + 1]
        brow = brow_all[f0 + h:f0 + h + 1, :]
        irow = smt[i0 + h:i0 + h + 1, :]
        icol = sm[:, i0 + h:i0 + h + 1]
        m0 = ms_ref[h][:, 0:1]
        dmat = jnp.where(causal, bcol - brow + irow, NEG)
        inter = m0 + bcol
        m = jnp.maximum(jnp.max(dmat, axis=-1, keepdims=True), inter)
        w = jnp.exp(dmat - m)
        a = jnp.exp(inter - m)
        ws = w * _dot_nt(qh, kh)
        c_prev = cs_ref[h]
        n_prev = ns_ref[h]
        num = _dot(ws.astype(BF16), vh) + a * _dot(qh, c_prev.astype(BF16))
        den = jnp.sum(ws, axis=-1, keepdims=True) + a * jnp.sum(qh.astype(F32) * n_prev, axis=-1, keepdims=True)
        hh = num / jnp.maximum(jnp.abs(den), jnp.exp(-m))
        hn = hh * lax.rsqrt(jnp.mean(hh * hh, axis=-1, keepdims=True) + EPS)
        h_ref[0, :, sl] = hn.astype(h_ref.dtype)
        m_last = m[chunk - 1:chunk, :]
        b_last = bcol[chunk - 1:chunk, :]
        w_last = jnp.exp(b_last - bcol + icol - m_last)
        a_last = jnp.exp(m0 + b_last - m_last)
        kw = kh.astype(F32) * w_last
        cs_ref[h] = a_last * c_prev + _dot_tn(kw.astype(BF16), vh)
        ns_ref[h] = a_last * n_prev + jnp.sum(kw, axis=0, keepdims=True)
        ms_ref[h] = jnp.broadcast_to(m_last, (1, LANES))

    @pl.when(ci == pl.num_programs(1) - 1)
    def _():
        c_ref[0] = cs_ref[...]
        n_ref[0] = ns_ref[...]
        m_ref[0] = ms_ref[...]


def _mlstm(q, k, v, small, small_t, c0, n0, m0, *, chunk):
    bsz, seq, _ = q.shape
    nc = seq // chunk
    r = lax.broadcasted_iota(jnp.int32, (chunk, chunk), 0)
    c = lax.broadcasted_iota(jnp.int32, (chunk, chunk), 1)
    ltri = (c <= r).astype(F32)
    utri = (r <= c).astype(F32)
    tokb = pl.BlockSpec((1, chunk, B_WIDTH), lambda b, i: (b, i, 0))
    st = lambda shp: pl.BlockSpec((1,) + shp, lambda b, i: (b,) + (0,) * len(shp))
    return pl.pallas_call(
        functools.partial(_mlstm_kernel, chunk=chunk),
        grid=(bsz, nc),
        in_specs=[tokb, tokb, tokb,
                  pl.BlockSpec((1, chunk, SMALL_W), lambda b, i: (b, i, 0)),
                  pl.BlockSpec((1, 16, chunk), lambda b, i: (b, 0, i)),
                  _resident((chunk, chunk)), _resident((chunk, chunk)),
                  st((B_HEADS, B_DIM, B_DIM)), st((B_HEADS, 1, B_DIM)), st((B_HEADS, 1, LANES))],
        out_specs=[tokb, st((B_HEADS, B_DIM, B_DIM)), st((B_HEADS, 1, B_DIM)), st((B_HEADS, 1, LANES))],
        out_shape=[jax.ShapeDtypeStruct((bsz, seq, B_WIDTH), BF16),
                   jax.ShapeDtypeStruct((bsz, B_HEADS, B_DIM, B_DIM), F32),
                   jax.ShapeDtypeStruct((bsz, B_HEADS, 1, B_DIM), F32),
                   jax.ShapeDtypeStruct((bsz, B_HEADS, 1, LANES), F32)],
        scratch_shapes=[pltpu.VMEM((B_HEADS, B_DIM, B_DIM), F32), pltpu.VMEM((B_HEADS, 1, B_DIM), F32),
                        pltpu.VMEM((B_HEADS, 1, LANES), F32)],
        compiler_params=pltpu.CompilerParams(dimension_semantics=("arbitrary", "arbitrary"),
                                             vmem_limit_bytes=VMEM_LIMIT),
        name="mlstm",
    )(q, k, v, small, small_t, ltri, utri, c0, n0, m0)


def _postmix_kernel(ha_ref, hb_ref, ob_ref, sga_ref, sgb_ref, x_ref, mod_ref, ghb_ref, g2_ref,
                    wpa_ref, wpb_ref, wo_ref, wpq_ref, x1_ref, h2_ref, pq_ref):
    mod = mod_ref[0]
    gt1 = mod[:, 2 * D_MODEL:3 * D_MODEL]
    sh2 = mod[:, 3 * D_MODEL:4 * D_MODEL]
    sc2 = mod[:, 4 * D_MODEL:5 * D_MODEL]
    hb = (hb_ref[0].astype(F32) * ghb_ref[...] * ob_ref[0].astype(F32)).astype(BF16)
    merged = (sga_ref[0].astype(F32) * _dot(ha_ref[0], wpa_ref[...])
              + sgb_ref[0].astype(F32) * _dot(hb, wpb_ref[...]))
    x1 = x_ref[0] + gt1 * _dot(merged.astype(BF16), wo_ref[...])
    x1_ref[0] = x1
    h2 = x1 * lax.rsqrt(jnp.mean(x1 * x1, axis=-1, keepdims=True) + EPS)
    h2 = h2 * g2_ref[...] * (1.0 + sc2) + sh2
    h2_ref[0] = h2
    pq_ref[0] = _dot(h2.astype(BF16), wpq_ref[...]).astype(BF16)


def _postmix(ha, hb, ob, sga, sgb, x, mod, ghb, g2, wpa, wpb, wo, wpq, tm):
    bsz, seq, _ = x.shape
    tok = lambda w: pl.BlockSpec((1, tm, w), lambda b, i: (b, i, 0))
    consts = [ghb, g2, wpa, wpb, wo, wpq]
    return pl.pallas_call(
        _postmix_kernel,
        grid=(bsz, seq // tm),
        in_specs=[tok(A_WIDTH), tok(B_WIDTH), tok(B_WIDTH), tok(D_MODEL), tok(D_MODEL), tok(D_MODEL),
                  pl.BlockSpec((1, 1, mod.shape[-1]), lambda b, i: (b, 0, 0))]
                 + [_resident(c.shape) for c in consts],
        out_specs=[tok(D_MODEL), tok(D_MODEL), tok(D_MODEL)],
        out_shape=[jax.ShapeDtypeStruct((bsz, seq, D_MODEL), F32), jax.ShapeDtypeStruct((bsz, seq, D_MODEL), F32),
                   jax.ShapeDtypeStruct((bsz, seq, D_MODEL), BF16)],
        compiler_params=pltpu.CompilerParams(dimension_semantics=("arbitrary", "arbitrary"),
                                             vmem_limit_bytes=VMEM_LIMIT),
        name="postmix",
    )(ha, hb, ob, sga, sgb, x, mod, *consts)


def _top16(x, ids):
    big = jnp.int32(2**30)
    vals, idxs = [], []
    for _ in range(P_TOPK):
        mx = jnp.max(x, axis=0, keepdims=True)
        am = jnp.min(jnp.where(x == mx, ids, big), axis=0, keepdims=True)
        vals.append(mx)
        idxs.append(am)
        x = jnp.where(ids == am, -jnp.inf, x)
    return jnp.concatenate(vals, axis=0), jnp.concatenate(idxs, axis=0)


def _pair_candidates(a, b):
    t = a.shape[1]
    row16 = lax.broadcasted_iota(jnp.int32, (P_TOPK, t), 0)
    row8 = lax.broadcasted_iota(jnp.int32, (SUBLANES, t), 0)
    vals = [a[0:1, :] + b]
    ids = [row16]
    for i in range(1, SUBLANES):
        vals.append(a[i:i + 1, :] + b[0:SUBLANES, :])
        ids.append(row8 + P_TOPK * i)
    vals.append(a[SUBLANES:, :] + b[0:1, :])
    ids.append((row8 + SUBLANES) * P_TOPK)
    return jnp.concatenate(vals, axis=0), jnp.concatenate(ids, axis=0)


def _pick(table, sel):
    out = jnp.zeros_like(table)
    for a in range(P_TOPK):
        out = jnp.where(sel == a, table[a:a + 1, :], out)
    return out


def _topk_kernel(pq_ref, sk_ref, idx_ref, g_ref):
    pq = pq_ref[...]
    key_ids = lax.broadcasted_iota(jnp.int32, (P_KEYS, pq.shape[0]), 0)
    for h in range(P_HEADS):
        sv, si = [], []
        for p in range(2):
            j = 2 * h + p
            s = _dot_nt(sk_ref[j], pq[:, j * P_HALF:(j + 1) * P_HALF])
            v, i = _top16(s, key_ids)
            sv.append(v)
            si.append(i)
        fv, fi = _top16(*_pair_candidates(sv[0], sv[1]))
        e = _pick(si[0], fi >> 4) * P_KEYS + _pick(si[1], fi & (P_TOPK - 1))
        ex = jnp.exp(fv - fv[0:1, :])
        g = ex / jnp.sum(ex, axis=0, keepdims=True)
        idx_ref[h * P_TOPK:(h + 1) * P_TOPK, :] = e
        g_ref[h * P_TOPK:(h + 1) * P_TOPK, :] = g


def _peer_topk(pq, sub_keys, tt):
    t = pq.shape[0]
    return pl.pallas_call(
        _topk_kernel,
        grid=(t // tt,),
        in_specs=[pl.BlockSpec((tt, D_MODEL), lambda i: (i, 0)), _resident(sub_keys.shape)],
        out_specs=[pl.BlockSpec((P_SEL, tt), lambda i: (0, i)), pl.BlockSpec((P_SEL, tt), lambda i: (0, i))],
        out_shape=[jax.ShapeDtypeStruct((P_SEL, t), jnp.int32), jax.ShapeDtypeStruct((P_SEL, t), F32)],
        compiler_params=pltpu.CompilerParams(dimension_semantics=("arbitrary",), vmem_limit_bytes=VMEM_LIMIT),
        name="peer_topk",
    )(pq, sub_keys)


def _pack_table(t):
    e, d = t.shape
    bits = lax.bitcast_convert_type(t.astype(BF16), jnp.uint16).astype(jnp.uint32)
    bits = bits.reshape(e, 2, d // (2 * LANES), LANES)
    return bits[:, 0] | (bits[:, 1] << 16)


def _halves(word):
    lo = pltpu.bitcast(word << 16, F32)
    hi = pltpu.bitcast(word & jnp.uint32(0xFFFF0000), F32)
    return lo, hi


def _merge_sublanes(a, b, step):
    sub = lax.broadcasted_iota(jnp.int32, (SUBLANES, LANES), 0)
    low = (sub & step) == 0
    return jnp.where(low, a + pltpu.roll(a, SUBLANES - step, axis=0), b + pltpu.roll(b, step, axis=0))


def _fold8(parts):
    pair = lambda i, j: jnp.concatenate([parts[i], parts[j]], axis=0)
    even = _merge_sublanes(pair(0, 4), pair(2, 6), 2)
    odd = _merge_sublanes(pair(1, 5), pair(3, 7), 2)
    return _merge_sublanes(even, odd, 1)


def _peer_act_kernel(e_ref, x_ref, g_ref, tab_ref, w_ref, q_ref, *, tb):
    def tok(t, carry):
        x = x_ref[t]
        xlo, xhi = x[0:4], x[4:8]
        for grp in range(P_SEL // SUBLANES):
            parts = []
            for j in range(SUBLANES):
                lo, hi = _halves(tab_ref[e_ref[t, grp * SUBLANES + j]])
                parts.append(lo * xlo + hi * xhi)
            q_ref[t, grp * SUBLANES:(grp + 1) * SUBLANES, :] = _fold8(parts)
        return carry
    lax.fori_loop(0, tb, tok, 0)

    kk = lax.broadcasted_iota(jnp.int32, (2 * LANES, 2 * LANES), 0)
    nn = lax.broadcasted_iota(jnp.int32, (2 * LANES, 2 * LANES), 1)
    ones = jnp.where((kk < LANES) == (nn < LANES), 1.0, 0.0).astype(BF16)
    lane = lax.broadcasted_iota(jnp.int32, (P_SEL, LANES), 1)
    act = jnp.zeros((P_SEL, tb), F32)
    for t in range(0, tb, 2):
        qh, ql = _split2(jnp.concatenate([q_ref[t], q_ref[t + 1]], axis=1))
        r = _dot(qh, ones) + _dot(ql, ones)
        act = jnp.where(lane == t, r[:, 0:LANES], act)
        act = jnp.where(lane == t + 1, r[:, LANES:], act)
    gelu = 0.5 * act * (1.0 + lax.erf(act * (2.0 ** -0.5)))
    w_ref[...] = g_ref[...] * gelu


def _peer_act(e_tok, x3, g_t, table, tb):
    t = x3.shape[0]
    return pl.pallas_call(
        functools.partial(_peer_act_kernel, tb=tb),
        grid=(t // tb,),
        in_specs=[pl.BlockSpec((tb, P_SEL), lambda i: (i, 0), memory_space=pltpu.SMEM),
                  pl.BlockSpec((tb, SUBLANES, LANES), lambda i: (i, 0, 0)),
                  pl.BlockSpec((P_SEL, tb), lambda i: (0, i)),
                  _resident(table.shape)],
        out_specs=pl.BlockSpec((P_SEL, tb), lambda i: (0, i)),
        out_shape=jax.ShapeDtypeStruct((P_SEL, t), F32),
        scratch_shapes=[pltpu.VMEM((tb, P_SEL, LANES), F32)],
        compiler_params=pltpu.CompilerParams(dimension_semantics=("arbitrary",),
                                             vmem_limit_bytes=TABLE_VMEM_LIMIT),
        name="peer_act",
    )(e_tok, x3, g_t, table)


def _peer_out_kernel(e_ref, w_ref, x1_ref, gt_ref, tab_ref, o_ref, wb_ref, *, tb, group):
    wh, wl = _split2(w_ref[...])
    rows = lax.broadcasted_iota(jnp.int32, (tb, 2 * LANES), 0)
    second = lax.broadcasted_iota(jnp.int32, (tb, 2 * LANES), 1) >= LANES
    for t in range(0, tb, 2):
        onehot = jnp.where(rows == jnp.where(second, t + 1, t), 1.0, 0.0).astype(BF16)
        both = _dot(wh, onehot) + _dot(wl, onehot)
        wb_ref[t] = both[:, 0:LANES]
        wb_ref[t + 1] = both[:, LANES:]

    def tok(t, carry):
        lo_acc = [jnp.zeros((4, LANES), F32), jnp.zeros((4, LANES), F32)]
        hi_acc = [jnp.zeros((4, LANES), F32), jnp.zeros((4, LANES), F32)]
        for c in range(P_SEL):
            lo, hi = _halves(tab_ref[e_ref[t, c]])
            w = jnp.broadcast_to(wb_ref[t, c:c + 1, :], (4, LANES))
            lo_acc[c % 2] = lo_acc[c % 2] + lo * w
            hi_acc[c % 2] = hi_acc[c % 2] + hi * w
        y = jnp.concatenate([lo_acc[0] + lo_acc[1], hi_acc[0] + hi_acc[1]], axis=0)
        o_ref[t] = x1_ref[t] + gt_ref[t // group] * y
        return carry
    lax.fori_loop(0, tb, tok, 0)


def _peer_out(e_tok, w_t, x1_3, gt_groups, table, tb, group):
    t = x1_3.shape[0]
    return pl.pallas_call(
        functools.partial(_peer_out_kernel, tb=tb, group=group),
        grid=(t // tb,),
        in_specs=[pl.BlockSpec((tb, P_SEL), lambda i: (i, 0), memory_space=pltpu.SMEM),
                  pl.BlockSpec((P_SEL, tb), lambda i: (0, i)),
                  pl.BlockSpec((tb, SUBLANES, LANES), lambda i: (i, 0, 0)),
                  pl.BlockSpec((tb // group, SUBLANES, LANES), lambda i: (i, 0, 0)),
                  _resident(table.shape)],
        out_specs=pl.BlockSpec((tb, SUBLANES, LANES), lambda i: (i, 0, 0)),
        out_shape=jax.ShapeDtypeStruct((t, SUBLANES, LANES), F32),
        scratch_shapes=[pltpu.VMEM((tb, P_SEL, LANES), F32)],
        compiler_params=pltpu.CompilerParams(dimension_semantics=("arbitrary",),
                                             vmem_limit_bytes=TABLE_VMEM_LIMIT),
        name="peer_out",
    )(e_tok, w_t, x1_3, gt_groups, table)


def _prep_weights(w_in, b_in, g_qa, g_ka):
    sizes = (A_WIDTH, A_WIDTH, A_WIDTH, A_HEADS, B_WIDTH, B_WIDTH, B_WIDTH, B_HEADS, B_HEADS, B_WIDTH,
             D_MODEL, D_MODEL)
    offs = [0]
    for s in sizes:
        offs.append(offs[-1] + s)
    col = lambda a, i: a[..., offs[i]:offs[i + 1]]
    cat = lambda a, ids: jnp.concatenate([col(a, i) for i in ids], axis=-1)
    wa, ba = cat(w_in, (0, 1, 2)), cat(b_in, (0, 1, 2))
    wb, bb = cat(w_in, (4, 5, 6)), cat(b_in, (4, 5, 6))
    wg, bg = cat(w_in, (9, 10, 11)), cat(b_in, (9, 10, 11))
    ws, bs = cat(w_in, (3, 7, 8)), cat(b_in, (3, 7, 8))
    pad = SMALL_W - ws.shape[-1]
    ws = jnp.pad(ws, ((0, 0), (0, pad)))
    bs = jnp.pad(bs, ((0, pad),))
    wsh, wsl = _split2(ws)
    blk = lax.broadcasted_iota(jnp.int32, (A_WIDTH, A_WIDTH), 0) // A_HEAD_DIM
    blk_t = lax.broadcasted_iota(jnp.int32, (A_WIDTH, A_WIDTH), 1) // A_HEAD_DIM
    bd = jnp.where(blk == blk_t, 1.0 / A_HEAD_DIM, 0.0).astype(BF16)
    r = lambda v: v.reshape(1, -1)
    return (wa.astype(BF16), r(ba), wb.astype(BF16), r(bb), wg.astype(BF16), r(bg), wsh, wsl, r(bs),
            r(g_qa), r(g_ka), bd)


def _gate_rows(small):
    return jnp.swapaxes(small[..., 0:16], 1, 2)


def kernel(x_prompt, x_sample, c_prompt, c_sample, cache_k, cache_v, cache_logf, state_C, state_n, state_m,
           w_mod, b_mod, g_norm1, g_norm2, w_in, b_in, g_qa, g_ka, g_hb, w_pa, w_pb, w_o, w_pq, sub_keys,
           expert_u, expert_v):
    assert w_mod.shape[0] == 1, "single-layer step"
    bp, sp, _ = x_prompt.shape
    bs, ss, _ = x_sample.shape
    past = cache_k.shape[2]
    inw = _prep_weights(w_in[0], b_in[0], g_qa[0], g_ka[0])
    g1, g2 = g_norm1[0].reshape(1, -1), g_norm2[0].reshape(1, -1)
    ghb = g_hb[0].reshape(1, -1)
    post_w = (w_pa[0].astype(BF16), w_pb[0].astype(BF16), w_o[0].astype(BF16), w_pq[0].astype(BF16))
    skeys = sub_keys[0].reshape(2 * P_HEADS, P_KEYS, P_HALF).astype(BF16)
    tab_u, tab_v = _pack_table(expert_u[0]), _pack_table(expert_v[0])

    mod_p = _modulation(c_prompt, w_mod[0], b_mod[0])[:, None, :]
    mod_s = _modulation(c_sample, w_mod[0], b_mod[0])[:, None, :]

    tm = 256
    (qa, ka, va, qb, kb, vb, ob, sga, sgb, small) = _inproj(x_prompt, mod_p, g1, inw, tm)
    srow = _gate_rows(small)
    fk = _cumsum_lanes(srow[:, 0:A_HEADS, :], 512).reshape(bp, A_HEADS // 2, 2, sp)
    ha = _fox_attention(qa, ka, va, fk, tq=min(1024, sp), tk=min(1024, sp), qoff=0)
    zc = jnp.zeros((bp, B_HEADS, B_DIM, B_DIM), F32)
    zn = jnp.zeros((bp, B_HEADS, 1, B_DIM), F32)
    hb, c_p, n_p, m_p = _mlstm(qb, kb, vb, small, srow, zc, zn, zn, chunk=128)
    x1_p, h2_p, pq_p = _postmix(ha, hb, ob, sga, sgb, x_prompt, mod_p, ghb, g2, *post_w, tm)
    state_p = (ka.reshape(1, bp, sp, A_HEADS, A_HEAD_DIM), va.reshape(1, bp, sp, A_HEADS, A_HEAD_DIM),
               small[None, :, :, 0:A_HEADS], c_p[None], n_p.reshape(1, bp, B_HEADS, B_DIM),
               m_p[None, :, :, 0, 0])

    (qa, ka, va, qb, kb, vb, ob, sga, sgb, small) = _inproj(x_sample, mod_s, g1, inw, ss)
    s_all = past + ss
    s_pad = -(-s_all // LANES) * LANES
    padk = lambda a: jnp.pad(a, ((0, 0), (0, s_pad - s_all), (0, 0)))
    kk = padk(jnp.concatenate([cache_k[0].reshape(bs, past, A_WIDTH), ka], axis=1))
    vv = padk(jnp.concatenate([cache_v[0].reshape(bs, past, A_WIDTH), va], axis=1))
    lf_all = padk(jnp.concatenate([cache_logf[0], small[:, :, 0:A_HEADS]], axis=1))
    tc = max(t for t in (512, 384, 256, 128) if s_pad % t == 0)
    fk = _cumsum_lanes(jnp.swapaxes(lf_all, 1, 2), tc).reshape(bs, A_HEADS // 2, 2, s_pad)
    ha = _fox_attention(qa, kk, vv, fk, tq=ss, tk=s_pad, qoff=past)
    chunk = 128
    padt = lambda a: jnp.pad(a, ((0, 0), (0, chunk - ss), (0, 0)))
    colid = jnp.arange(SMALL_W)
    is_input = (colid >= A_HEADS) & (colid < A_HEADS + B_HEADS)
    real = (jnp.arange(chunk) < ss)[None, :, None]
    small_pad = jnp.where(real, padt(small), jnp.where(is_input, NEG, 0.0)[None, None, :])
    c0 = state_C[0].astype(F32)
    n0 = state_n[0].astype(F32).reshape(bs, B_HEADS, 1, B_DIM)
    m0 = jnp.broadcast_to(state_m[0].astype(F32)[:, :, None, None], (bs, B_HEADS, 1, LANES))
    hb, c_s, n_s, m_s = _mlstm(padt(qb), padt(kb), padt(vb), small_pad, _gate_rows(small_pad), c0, n0, m0,
                               chunk=chunk)
    x1_s, h2_s, pq_s = _postmix(ha, hb[:, 0:ss], ob, sga, sgb, x_sample, mod_s, ghb, g2, *post_w, ss)
    state_s = (ka.reshape(1, bs, ss, A_HEADS, A_HEAD_DIM), va.reshape(1, bs, ss, A_HEADS, A_HEAD_DIM),
               small[None, :, :, 0:A_HEADS], c_s[None], n_s.reshape(1, bs, B_HEADS, B_DIM),
               m_s[None, :, :, 0, 0])

    tp, ts = bp * sp, bs * ss
    tt = LANES
    assert tp % tt == 0 and ts % min(tt, ts) == 0
    idx_p, g_p = _peer_topk(pq_p.reshape(tp, D_MODEL), skeys, tt)
    idx_s, g_s = _peer_topk(pq_s.reshape(ts, D_MODEL), skeys, min(tt, ts))
    idx_t = jnp.concatenate([idx_p, idx_s], axis=1)
    g_t = jnp.concatenate([g_p, g_s], axis=1)
    e_tok = jnp.swapaxes(idx_t, 0, 1)
    tok3 = lambda a, b: jnp.concatenate([a.reshape(tp, SUBLANES, LANES), b.reshape(ts, SUBLANES, LANES)], axis=0)
    x3 = tok3(h2_p, h2_s)
    x1_3 = tok3(x1_p, x1_s)
    tb = LANES
    assert (tp + ts) % tb == 0 and sp % ss == 0 and tb % ss == 0
    w_t = _peer_act(e_tok, x3, g_t, tab_u, tb)
    group = ss
    gt2 = lambda mod, rep: jnp.repeat(mod[:, 0, 5 * D_MODEL:6 * D_MODEL], rep, axis=0)
    gt_groups = jnp.concatenate([gt2(mod_p, sp // group), gt2(mod_s, 1)], axis=0).reshape(-1, SUBLANES, LANES)
    y3 = _peer_out(e_tok, w_t, x1_3, gt_groups, tab_v, tb, group)
    y_p = y3[0:tp].reshape(bp, sp, D_MODEL)
    y_s = y3[tp:].reshape(bs, ss, D_MODEL)
    return (y_p, y_s) + state_p + state_s
```

```python
import functools

import jax
import jax.numpy as jnp
from jax import lax
from jax.experimental import pallas as pl
from jax.experimental.pallas import tpu as pltpu

EPS = 1e-6
NEG = -1e30
D_MODEL = 1024
A_HEADS = 8
A_HEAD_DIM = 64
A_WIDTH = A_HEADS * A_HEAD_DIM
B_HEADS = 4
B_DIM = 128
B_WIDTH = B_HEADS * B_DIM
P_HEADS = 8
P_KEYS = 128
P_HALF = 64
P_TOPK = 16
P_SEL = P_HEADS * P_TOPK
LANES = 128
SUBLANES = 8
SMALL_W = 128
VMEM_LIMIT = 48 * 2**20
TABLE_VMEM_LIMIT = 56 * 2**20
F32 = jnp.float32
BF16 = jnp.bfloat16


def _dot(a, b):
    return jnp.dot(a, b, preferred_element_type=F32)


def _dot_nt(a, b):
    return lax.dot_general(a, b, (((1,), (1,)), ((), ())), preferred_element_type=F32)


def _dot_tn(a, b):
    return lax.dot_general(a, b, (((0,), (0,)), ((), ())), preferred_element_type=F32)


def _dot_f32(a, b):
    return jnp.dot(a, b, preferred_element_type=F32, precision=lax.Precision.HIGHEST)


def _split2(x):
    hi = x.astype(BF16)
    lo = (x - hi.astype(F32)).astype(BF16)
    return hi, lo


def _log_sigmoid(x):
    return jnp.minimum(x, 0.0) - jnp.log(1.0 + jnp.exp(-jnp.abs(x)))


def _sigmoid(x):
    return 1.0 / (1.0 + jnp.exp(-x))


def _resident(shape):
    nd = len(shape)
    return pl.BlockSpec(shape, lambda *_: (0,) * nd, pipeline_mode=pl.Buffered(1))


def _mod_kernel(c_ref, w_ref, b_ref, o_ref):
    c = c_ref[...]
    s = c * _sigmoid(c)
    o_ref[...] = _dot_f32(s, w_ref[...]) + b_ref[...]


def _modulation(c, w_mod, b_mod):
    bsz = c.shape[0]
    n = w_mod.shape[1]
    tn = 1024
    return pl.pallas_call(
        _mod_kernel,
        grid=(n // tn,),
        in_specs=[pl.BlockSpec((bsz, D_MODEL), lambda j: (0, 0)),
                  pl.BlockSpec((D_MODEL, tn), lambda j: (0, j)),
                  pl.BlockSpec((1, tn), lambda j: (0, j))],
        out_specs=pl.BlockSpec((bsz, tn), lambda j: (0, j)),
        out_shape=jax.ShapeDtypeStruct((bsz, n), F32),
        compiler_params=pltpu.CompilerParams(dimension_semantics=("arbitrary",), vmem_limit_bytes=VMEM_LIMIT),
        name="modulation",
    )(c, w_mod, b_mod.reshape(1, n))


def _inproj_kernel(x_ref, mod_ref, g1_ref, wa_ref, ba_ref, wb_ref, bb_ref, wg_ref, bg_ref,
                   wsh_ref, wsl_ref, bs_ref, gq_ref, gk_ref, bd_ref,
                   qa_ref, ka_ref, va_ref, qb_ref, kb_ref, vb_ref, ob_ref, sga_ref, sgb_ref, small_ref):
    x = x_ref[0]
    mod = mod_ref[0]
    sh1 = mod[:, 0:D_MODEL]
    sc1 = mod[:, D_MODEL:2 * D_MODEL]
    h = x * lax.rsqrt(jnp.mean(x * x, axis=-1, keepdims=True) + EPS)
    h = h * g1_ref[...] * (1.0 + sc1) + sh1
    hh, hl = _split2(h)

    za = _dot(hh, wa_ref[...]) + ba_ref[...]
    bd = bd_ref[...]

    def head_norm(z):
        msq = _dot((z * z).astype(BF16), bd)
        return z * lax.rsqrt(msq + EPS)

    q = head_norm(za[:, 0:A_WIDTH]) * gq_ref[...]
    k = head_norm(za[:, A_WIDTH:2 * A_WIDTH]) * gk_ref[...]
    qa_ref[0] = (q * (LOG2E * A_HEAD_DIM ** -0.5)).astype(BF16)
    ka_ref[0] = k
    va_ref[0] = za[:, 2 * A_WIDTH:3 * A_WIDTH]

    zb = _dot(hh, wb_ref[...]) + bb_ref[...]
    qb_ref[0] = zb[:, 0:B_WIDTH].astype(BF16)
    kb_ref[0] = (zb[:, B_WIDTH:2 * B_WIDTH] * (B_DIM ** -0.5)).astype(BF16)
    vb_ref[0] = zb[:, 2 * B_WIDTH:3 * B_WIDTH].astype(BF16)

    zg = _sigmoid(_dot(hh, wg_ref[...]) + bg_ref[...])
    ob_ref[0] = zg[:, 0:B_WIDTH].astype(BF16)
    sga_ref[0] = zg[:, B_WIDTH:B_WIDTH + D_MODEL].astype(BF16)
    sgb_ref[0] = zg[:, B_WIDTH + D_MODEL:B_WIDTH + 2 * D_MODEL].astype(BF16)

    zs = _dot(hh, wsh_ref[...]) + _dot(hl, wsh_ref[...]) + _dot(hh, wsl_ref[...]) + bs_ref[...]
    col = lax.broadcasted_iota(jnp.int32, zs.shape, 1)
    is_input_gate = (col >= A_HEADS) & (col < A_HEADS + B_HEADS)
    small_ref[0] = jnp.where(is_input_gate, zs, _log_sigmoid(zs))


def _inproj(x, mod, g1, wts, tm):
    bsz, seq, _ = x.shape
    (wa, ba, wb, bb, wg, bg, wsh, wsl, bs, gq, gk, bd) = wts
    tok = lambda w, dt: (pl.BlockSpec((1, tm, w), lambda b, i: (b, i, 0)), jax.ShapeDtypeStruct((bsz, seq, w), dt))
    outs = [tok(A_WIDTH, BF16), tok(A_WIDTH, F32), tok(A_WIDTH, F32),
            tok(B_WIDTH, BF16), tok(B_WIDTH, BF16), tok(B_WIDTH, BF16),
            tok(B_WIDTH, BF16), tok(D_MODEL, BF16), tok(D_MODEL, BF16), tok(SMALL_W, F32)]
    consts = [g1, wa, ba, wb, bb, wg, bg, wsh, wsl, bs, gq, gk, bd]
    return pl.pallas_call(
        _inproj_kernel,
        grid=(bsz, seq // tm),
        in_specs=[pl.BlockSpec((1, tm, D_MODEL), lambda b, i: (b, i, 0)),
                  pl.BlockSpec((1, 1, mod.shape[-1]), lambda b, i: (b, 0, 0))]
                 + [_resident(c.shape) for c in consts],
        out_specs=[o[0] for o in outs],
        out_shape=[o[1] for o in outs],
        compiler_params=pltpu.CompilerParams(dimension_semantics=("arbitrary", "arbitrary"),
                                             vmem_limit_bytes=VMEM_LIMIT),
        name="inproj",
    )(x, mod, *consts)


def _cumsum_kernel(x_ref, u_ref, o_ref, carry_ref):
    @pl.when(pl.program_id(1) == 0)
    def _():
        carry_ref[...] = jnp.zeros_like(carry_ref)
    tc = x_ref.shape[-1]
    f = _dot_f32(x_ref[0], u_ref[...]) + carry_ref[:, 0:1]
    o_ref[0] = f
    carry_ref[...] = jnp.broadcast_to(f[:, tc - 1:tc], carry_ref.shape)


def _cumsum_lanes(x, tc):
    bsz, rows, seq = x.shape
    upper = (lax.broadcasted_iota(jnp.int32, (tc, tc), 0) <= lax.broadcasted_iota(jnp.int32, (tc, tc), 1)).astype(F32)
    return pl.pallas_call(
        _cumsum_kernel,
        grid=(bsz, seq // tc),
        in_specs=[pl.BlockSpec((1, rows, tc), lambda b, i: (b, 0, i)), _resident((tc, tc))],
        out_specs=pl.BlockSpec((1, rows, tc), lambda b, i: (b, 0, i)),
        out_shape=jax.ShapeDtypeStruct((bsz, rows, seq), F32),
        scratch_shapes=[pltpu.VMEM((rows, LANES), F32)],
        compiler_params=pltpu.CompilerParams(dimension_semantics=("arbitrary", "arbitrary"),
                                             vmem_limit_bytes=VMEM_LIMIT),
        name="cumsum",
    )(x, upper)


LOG2E = 1.4426950408889634


def _fox_kernel(qt_ref, kt_ref, q_ref, k_ref, v_ref, f_ref, o_ref, m_ref, l_ref, acc_ref, *, tq, tk, qoff):
    step_id = pl.program_id(2)
    qi = qt_ref[step_id]
    ki = kt_ref[step_id]
    q_lo = qoff + qi * tq
    last = (q_lo + tq - 1) // tk

    @pl.when(ki == 0)
    def _():
        m_ref[...] = jnp.full_like(m_ref, NEG)
        l_ref[...] = jnp.zeros_like(l_ref)
        acc_ref[...] = jnp.zeros_like(acc_ref)

    def step(masked):
        q = q_ref[0]
        k = k_ref[0].astype(BF16)
        v = v_ref[0].astype(BF16)
        lane = lax.broadcasted_iota(jnp.int32, (tq, LANES), 1)
        first_head = lane < A_HEAD_DIM
        alphas, pvs = [], []
        for h in range(2):
            qh = jnp.where(first_head if h == 0 else ~first_head, q, jnp.zeros_like(q))
            s = _dot_nt(qh, k) - f_ref[0, 0, h:h + 1, :] * LOG2E
            if masked:
                kpos = ki * tk + lax.broadcasted_iota(jnp.int32, (tq, tk), 1)
                qpos = q_lo + lax.broadcasted_iota(jnp.int32, (tq, tk), 0)
                s = jnp.where(kpos <= qpos, s, NEG)
            m_prev = m_ref[h]
            m_new = jnp.maximum(m_prev, jnp.max(s, axis=-1, keepdims=True))
            alpha = jnp.exp2(m_prev - m_new)
            p = jnp.exp2(s - jnp.concatenate([m_new] * (tk // LANES), axis=1))
            l_ref[h] = alpha * l_ref[h] + jnp.sum(p, axis=-1, keepdims=True)
            m_ref[h] = m_new
            alphas.append(alpha)
            pvs.append(_dot(p.astype(BF16), v))
        acc_ref[...] = (acc_ref[...] * jnp.where(first_head, alphas[0], alphas[1])
                        + jnp.where(first_head, pvs[0], pvs[1]))

    needs_mask = (ki + 1) * tk - 1 > q_lo

    @pl.when(needs_mask)
    def _():
        step(True)

    @pl.when(jnp.logical_not(needs_mask))
    def _():
        step(False)

    @pl.when(ki == last)
    def _():
        lane = lax.broadcasted_iota(jnp.int32, (tq, LANES), 1)
        inv = jnp.where(lane < A_HEAD_DIM, 1.0 / l_ref[0], 1.0 / l_ref[1])
        o_ref[0] = (acc_ref[...] * inv).astype(o_ref.dtype)


def _fox_attention(q, k, v, fk, *, tq, tk, qoff):
    bsz, lq, _ = q.shape
    nq = lq // tq
    pairs = A_HEADS // 2
    sched = [(qi, ki) for qi in range(nq) for ki in range((qoff + qi * tq + tq - 1) // tk + 1)]
    qt = jnp.asarray([s[0] for s in sched], jnp.int32)
    kt = jnp.asarray([s[1] for s in sched], jnp.int32)
    grid_spec = pltpu.PrefetchScalarGridSpec(
        num_scalar_prefetch=2,
        grid=(bsz, pairs, len(sched)),
        in_specs=[pl.BlockSpec((1, tq, LANES), lambda b, hp, s, qt, kt: (b, qt[s], hp)),
                  pl.BlockSpec((1, tk, LANES), lambda b, hp, s, qt, kt: (b, kt[s], hp)),
                  pl.BlockSpec((1, tk, LANES), lambda b, hp, s, qt, kt: (b, kt[s], hp)),
                  pl.BlockSpec((1, 1, 2, tk), lambda b, hp, s, qt, kt: (b, hp, 0, kt[s]))],
        out_specs=pl.BlockSpec((1, tq, LANES), lambda b, hp, s, qt, kt: (b, qt[s], hp)),
        scratch_shapes=[pltpu.VMEM((2, tq, LANES), F32), pltpu.VMEM((2, tq, LANES), F32),
                        pltpu.VMEM((tq, LANES), F32)])
    return pl.pallas_call(
        functools.partial(_fox_kernel, tq=tq, tk=tk, qoff=qoff),
        grid_spec=grid_spec,
        out_shape=jax.ShapeDtypeStruct((bsz, lq, A_WIDTH), BF16),
        compiler_params=pltpu.CompilerParams(
            dimension_semantics=("arbitrary", "arbitrary", "arbitrary"), vmem_limit_bytes=VMEM_LIMIT),
        name="fox_attention",
    )(qt, kt, q, k, v, fk)


def _mlstm_kernel(q_ref, k_ref, v_ref, sm_ref, smt_ref, ltri_ref, utri_ref, c0_ref, n0_ref, m0_ref,
                  h_ref, c_ref, n_ref, m_ref, cs_ref, ns_ref, ms_ref, *, chunk):
    ci = pl.program_id(1)

    @pl.when(ci == 0)
    def _():
        cs_ref[...] = c0_ref[0]
        ns_ref[...] = n0_ref[0]
        ms_ref[...] = m0_ref[0]

    sm = sm_ref[0]
    smt = smt_ref[0]
    bcol_all = _dot_f32(ltri_ref[...], sm)
    brow_all = _dot_f32(smt, utri_ref[...])
    row = lax.broadcasted_iota(jnp.int32, (chunk, chunk), 0)
    colx = lax.broadcasted_iota(jnp.int32, (chunk, chunk), 1)
    causal = colx <= row
    i0 = A_HEADS
    f0 = A_HEADS + B_HEADS
    for h in range(B_HEADS):
        sl = slice(h * B_DIM, (h + 1) * B_DIM)
        qh, kh, vh = q_ref[0, :, sl], k_ref[0, :, sl], v_ref[0, :, sl]
        bcol = bcol_all[:, f0 + h:f0 + h + 1]
        brow = brow_all[f0 + h:f0 + h + 1, :]
        irow = smt[i0 + h:i0 + h + 1, :]
        icol = sm[:, i0 + h:i0 + h + 1]
        m0 = ms_ref[h][:, 0:1]
        dmat = jnp.where(causal, bcol - brow + irow, NEG)
        inter = m0 + bcol
        m = jnp.maximum(jnp.max(dmat, axis=-1, keepdims=True), inter)
        w = jnp.exp(dmat - m)
        a = jnp.exp(inter - m)
        ws = w * _dot_nt(qh, kh)
        c_prev = cs_ref[h]
        n_prev = ns_ref[h]
        num = _dot(ws.astype(BF16), vh) + a * _dot(qh, c_prev.astype(BF16))
        den = jnp.sum(ws, axis=-1, keepdims=True) + a * jnp.sum(qh.astype(F32) * n_prev, axis=-1, keepdims=True)
        hh = num / jnp.maximum(jnp.abs(den), jnp.exp(-m))
        hn = hh * lax.rsqrt(jnp.mean(hh * hh, axis=-1, keepdims=True) + EPS)
        h_ref[0, :, sl] = hn.astype(h_ref.dtype)
        m_last = m[chunk - 1:chunk, :]
        b_last = bcol[chunk - 1:chunk, :]
        w_last = jnp.exp(b_last - bcol + icol - m_last)
        a_last = jnp.exp(m0 + b_last - m_last)
        kw = kh.astype(F32) * w_last
        cs_ref[h] = a_last * c_prev + _dot_tn(kw.astype(BF16), vh)
        ns_ref[h] = a_last * n_prev + jnp.sum(kw, axis=0, keepdims=True)
        ms_ref[h] = jnp.broadcast_to(m_last, (1, LANES))

    @pl.when(ci == pl.num_programs(1) - 1)
    def _():
        c_ref[0] = cs_ref[...]
        n_ref[0] = ns_ref[...]
        m_ref[0] = ms_ref[...]


def _mlstm(q, k, v, small, small_t, c0, n0, m0, *, chunk):
    bsz, seq, _ = q.shape
    nc = seq // chunk
    r = lax.broadcasted_iota(jnp.int32, (chunk, chunk), 0)
    c = lax.broadcasted_iota(jnp.int32, (chunk, chunk), 1)
    ltri = (c <= r).astype(F32)
    utri = (r <= c).astype(F32)
    tokb = pl.BlockSpec((1, chunk, B_WIDTH), lambda b, i: (b, i, 0))
    st = lambda shp: pl.BlockSpec((1,) + shp, lambda b, i: (b,) + (0,) * len(shp))
    return pl.pallas_call(
        functools.partial(_mlstm_kernel, chunk=chunk),
        grid=(bsz, nc),
        in_specs=[tokb, tokb, tokb,
                  pl.BlockSpec((1, chunk, SMALL_W), lambda b, i: (b, i, 0)),
                  pl.BlockSpec((1, 16, chunk), lambda b, i: (b, 0, i)),
                  _resident((chunk, chunk)), _resident((chunk, chunk)),
                  st((B_HEADS, B_DIM, B_DIM)), st((B_HEADS, 1, B_DIM)), st((B_HEADS, 1, LANES))],
        out_specs=[tokb, st((B_HEADS, B_DIM, B_DIM)), st((B_HEADS, 1, B_DIM)), st((B_HEADS, 1, LANES))],
        out_shape=[jax.ShapeDtypeStruct((bsz, seq, B_WIDTH), BF16),
                   jax.ShapeDtypeStruct((bsz, B_HEADS, B_DIM, B_DIM), F32),
                   jax.ShapeDtypeStruct((bsz, B_HEADS, 1, B_DIM), F32),
                   jax.ShapeDtypeStruct((bsz, B_HEADS, 1, LANES), F32)],
        scratch_shapes=[pltpu.VMEM((B_HEADS, B_DIM, B_DIM), F32), pltpu.VMEM((B_HEADS, 1, B_DIM), F32),
                        pltpu.VMEM((B_HEADS, 1, LANES), F32)],
        compiler_params=pltpu.CompilerParams(dimension_semantics=("arbitrary", "arbitrary"),
                                             vmem_limit_bytes=VMEM_LIMIT),
        name="mlstm",
    )(q, k, v, small, small_t, ltri, utri, c0, n0, m0)


def _postmix_kernel(ha_ref, hb_ref, ob_ref, sga_ref, sgb_ref, x_ref, mod_ref, ghb_ref, g2_ref,
                    wpa_ref, wpb_ref, wo_ref, wpq_ref, x1_ref, h2_ref, pq_ref):
    mod = mod_ref[0]
    gt1 = mod[:, 2 * D_MODEL:3 * D_MODEL]
    sh2 = mod[:, 3 * D_MODEL:4 * D_MODEL]
    sc2 = mod[:, 4 * D_MODEL:5 * D_MODEL]
    hb = (hb_ref[0].astype(F32) * ghb_ref[...] * ob_ref[0].astype(F32)).astype(BF16)
    merged = (sga_ref[0].astype(F32) * _dot(ha_ref[0], wpa_ref[...])
              + sgb_ref[0].astype(F32) * _dot(hb, wpb_ref[...]))
    x1 = x_ref[0] + gt1 * _dot(merged.astype(BF16), wo_ref[...])
    x1_ref[0] = x1
    h2 = x1 * lax.rsqrt(jnp.mean(x1 * x1, axis=-1, keepdims=True) + EPS)
    h2 = h2 * g2_ref[...] * (1.0 + sc2) + sh2
    h2_ref[0] = h2
    pq_ref[0] = _dot(h2.astype(BF16), wpq_ref[...]).astype(BF16)


def _postmix(ha, hb, ob, sga, sgb, x, mod, ghb, g2, wpa, wpb, wo, wpq, tm):
    bsz, seq, _ = x.shape
    tok = lambda w: pl.BlockSpec((1, tm, w), lambda b, i: (b, i, 0))
    consts = [ghb, g2, wpa, wpb, wo, wpq]
    return pl.pallas_call(
        _postmix_kernel,
        grid=(bsz, seq // tm),
        in_specs=[tok(A_WIDTH), tok(B_WIDTH), tok(B_WIDTH), tok(D_MODEL), tok(D_MODEL), tok(D_MODEL),
                  pl.BlockSpec((1, 1, mod.shape[-1]), lambda b, i: (b, 0, 0))]
                 + [_resident(c.shape) for c in consts],
        out_specs=[tok(D_MODEL), tok(D_MODEL), tok(D_MODEL)],
        out_shape=[jax.ShapeDtypeStruct((bsz, seq, D_MODEL), F32), jax.ShapeDtypeStruct((bsz, seq, D_MODEL), F32),
                   jax.ShapeDtypeStruct((bsz, seq, D_MODEL), BF16)],
        compiler_params=pltpu.CompilerParams(dimension_semantics=("arbitrary", "arbitrary"),
                                             vmem_limit_bytes=VMEM_LIMIT),
        name="postmix",
    )(ha, hb, ob, sga, sgb, x, mod, *consts)


def _top16(x, ids):
    big = jnp.int32(2**30)
    vals, idxs = [], []
    for _ in range(P_TOPK):
        mx = jnp.max(x, axis=0, keepdims=True)
        am = jnp.min(jnp.where(x == mx, ids, big), axis=0, keepdims=True)
        vals.append(mx)
        idxs.append(am)
        x = jnp.where(ids == am, -jnp.inf, x)
    return jnp.concatenate(vals, axis=0), jnp.concatenate(idxs, axis=0)


def _pair_candidates(a, b):
    t = a.shape[1]
    row16 = lax.broadcasted_iota(jnp.int32, (P_TOPK, t), 0)
    row8 = lax.broadcasted_iota(jnp.int32, (SUBLANES, t), 0)
    vals = [a[0:1, :] + b]
    ids = [row16]
    for i in range(1, SUBLANES):
        vals.append(a[i:i + 1, :] + b[0:SUBLANES, :])
        ids.append(row8 + P_TOPK * i)
    vals.append(a[SUBLANES:, :] + b[0:1, :])
    ids.append((row8 + SUBLANES) * P_TOPK)
    return jnp.concatenate(vals, axis=0), jnp.concatenate(ids, axis=0)


def _pick(table, sel):
    out = jnp.zeros_like(table)
    for a in range(P_TOPK):
        out = jnp.where(sel == a, table[a:a + 1, :], out)
    return out


def _topk_kernel(pq_ref, sk_ref, idx_ref, g_ref):
    pq = pq_ref[...]
    key_ids = lax.broadcasted_iota(jnp.int32, (P_KEYS, pq.shape[0]), 0)
    for h in range(P_HEADS):
        sv, si = [], []
        for p in range(2):
            j = 2 * h + p
            s = _dot_nt(sk_ref[j], pq[:, j * P_HALF:(j + 1) * P_HALF])
            v, i = _top16(s, key_ids)
            sv.append(v)
            si.append(i)
        fv, fi = _top16(*_pair_candidates(sv[0], sv[1]))
        e = _pick(si[0], fi >> 4) * P_KEYS + _pick(si[1], fi & (P_TOPK - 1))
        ex = jnp.exp(fv - fv[0:1, :])
        g = ex / jnp.sum(ex, axis=0, keepdims=True)
        idx_ref[h * P_TOPK:(h + 1) * P_TOPK, :] = e
        g_ref[h * P_TOPK:(h + 1) * P_TOPK, :] = g


def _peer_topk(pq, sub_keys, tt):
    t = pq.shape[0]
    return pl.pallas_call(
        _topk_kernel,
        grid=(t // tt,),
        in_specs=[pl.BlockSpec((tt, D_MODEL), lambda i: (i, 0)), _resident(sub_keys.shape)],
        out_specs=[pl.BlockSpec((P_SEL, tt), lambda i: (0, i)), pl.BlockSpec((P_SEL, tt), lambda i: (0, i))],
        out_shape=[jax.ShapeDtypeStruct((P_SEL, t), jnp.int32), jax.ShapeDtypeStruct((P_SEL, t), F32)],
        compiler_params=pltpu.CompilerParams(dimension_semantics=("arbitrary",), vmem_limit_bytes=VMEM_LIMIT),
        name="peer_topk",
    )(pq, sub_keys)


PEER_SUB = 16


def _pack_table(t):
    e, d = t.shape
    return t.astype(BF16).reshape(e, d // LANES, LANES)


def _merge_sublanes(a, b, step):
    sub = lax.broadcasted_iota(jnp.int32, (SUBLANES, LANES), 0)
    low = (sub & step) == 0
    if 2 * step == SUBLANES:
        return jnp.where(low, a, b) + pltpu.roll(jnp.where(low, b, a), step, axis=0)
    return jnp.where(low, a + pltpu.roll(a, SUBLANES - step, axis=0), b + pltpu.roll(b, step, axis=0))


def _fold8(parts):
    quads = [_merge_sublanes(parts[i], parts[i + 4], 4) for i in range(4)]
    even = _merge_sublanes(quads[0], quads[2], 2)
    odd = _merge_sublanes(quads[1], quads[3], 2)
    return _merge_sublanes(even, odd, 1)


def _id_pipeline(e_hbm, ebuf, sem, body):
    step = pl.program_id(0)
    words = PEER_SUB * P_SEL
    nsub = LANES // PEER_SUB
    total = pl.num_programs(0) * nsub

    def copy(sub, slot):
        return pltpu.make_async_copy(e_hbm.at[pl.ds(sub * words, words)], ebuf.at[slot], sem.at[slot])

    @pl.when(step == 0)
    def _():
        copy(0, 0).start()

    def pair(j, carry):
        sub = step * nsub + 2 * j
        copy(sub, 0).wait()
        copy(sub + 1, 1).start()
        body(0, 2 * j * PEER_SUB)

        @pl.when(sub + 2 < total)
        def _():
            copy(sub + 2, 0).start()
        copy(sub + 1, 1).wait()
        body(1, (2 * j + 1) * PEER_SUB)
        return carry
    lax.fori_loop(0, nsub // 2, pair, 0)


def _peer_act_kernel(e_hbm, x_ref, g_ref, tab_ref, w_ref, q_ref, ebuf, sem, *, tb):
    def gather(slot, tok0):
        for k in range(PEER_SUB):
            x = x_ref[tok0 + k]
            for grp in range(P_SEL // SUBLANES):
                parts = [tab_ref[ebuf[slot, k * P_SEL + grp * SUBLANES + j]].astype(F32) * x
                         for j in range(SUBLANES)]
                q_ref[tok0 + k, grp * SUBLANES:(grp + 1) * SUBLANES, :] = _fold8(parts)
    _id_pipeline(e_hbm, ebuf, sem, gather)

    kk = lax.broadcasted_iota(jnp.int32, (2 * LANES, 2 * LANES), 0)
    nn = lax.broadcasted_iota(jnp.int32, (2 * LANES, 2 * LANES), 1)
    ones = jnp.where((kk < LANES) == (nn < LANES), 1.0, 0.0).astype(BF16)
    lane = lax.broadcasted_iota(jnp.int32, (P_SEL, LANES), 1)
    act = jnp.zeros((P_SEL, tb), F32)
    for t in range(0, tb, 2):
        qh, ql = _split2(jnp.concatenate([q_ref[t], q_ref[t + 1]], axis=1))
        r = _dot(qh, ones) + _dot(ql, ones)
        act = jnp.where(lane == t, r[:, 0:LANES], act)
        act = jnp.where(lane == t + 1, r[:, LANES:], act)
    gelu = 0.5 * act * (1.0 + lax.erf(act * (2.0 ** -0.5)))
    w_ref[...] = g_ref[...] * gelu


def _id_scratch():
    return [pltpu.SMEM((2, PEER_SUB * P_SEL), jnp.int32), pltpu.SemaphoreType.DMA((2,))]


def _peer_act(e_flat, x3, g_t, table, tb):
    t = x3.shape[0]
    assert tb == LANES and t % tb == 0
    return pl.pallas_call(
        functools.partial(_peer_act_kernel, tb=tb),
        grid=(t // tb,),
        in_specs=[pl.BlockSpec(memory_space=pl.ANY),
                  pl.BlockSpec((tb, SUBLANES, LANES), lambda i: (i, 0, 0)),
                  pl.BlockSpec((P_SEL, tb), lambda i: (0, i)),
                  _resident(table.shape)],
        out_specs=pl.BlockSpec((P_SEL, tb), lambda i: (0, i)),
        out_shape=jax.ShapeDtypeStruct((P_SEL, t), F32),
        scratch_shapes=[pltpu.VMEM((tb, P_SEL, LANES), F32)] + _id_scratch(),
        compiler_params=pltpu.CompilerParams(dimension_semantics=("arbitrary",),
                                             vmem_limit_bytes=TABLE_VMEM_LIMIT),
        name="peer_act",
    )(e_flat, x3, g_t, table)


def _peer_out_kernel(e_hbm, w_ref, x1_ref, gt_ref, tab_ref, o_ref, wb_ref, ebuf, sem, *, tb, group):
    wh, wl = _split2(w_ref[...])
    rows = lax.broadcasted_iota(jnp.int32, (tb, 2 * LANES), 0)
    second = lax.broadcasted_iota(jnp.int32, (tb, 2 * LANES), 1) >= LANES
    for t in range(0, tb, 2):
        onehot = jnp.where(rows == jnp.where(second, t + 1, t), 1.0, 0.0).astype(BF16)
        both = _dot(wh, onehot) + _dot(wl, onehot)
        wb_ref[t] = both[:, 0:LANES]
        wb_ref[t + 1] = both[:, LANES:]

    def gather(slot, tok0):
        for k in range(PEER_SUB):
            t = tok0 + k
            acc = [jnp.zeros((SUBLANES, LANES), F32), jnp.zeros((SUBLANES, LANES), F32)]
            for c in range(P_SEL):
                row = tab_ref[ebuf[slot, k * P_SEL + c]].astype(F32)
                w = jnp.broadcast_to(wb_ref[t, c:c + 1, :], (SUBLANES, LANES))
                acc[c % 2] = acc[c % 2] + row * w
            o_ref[t] = x1_ref[t] + gt_ref[t // group] * (acc[0] + acc[1])
    _id_pipeline(e_hbm, ebuf, sem, gather)


def _peer_out(e_flat, w_t, x1_3, gt_groups, table, tb, group):
    t = x1_3.shape[0]
    assert tb == LANES and t % tb == 0
    return pl.pallas_call(
        functools.partial(_peer_out_kernel, tb=tb, group=group),
        grid=(t // tb,),
        in_specs=[pl.BlockSpec(memory_space=pl.ANY),
                  pl.BlockSpec((P_SEL, tb), lambda i: (0, i)),
                  pl.BlockSpec((tb, SUBLANES, LANES), lambda i: (i, 0, 0)),
                  pl.BlockSpec((tb // group, SUBLANES, LANES), lambda i: (i, 0, 0)),
                  _resident(table.shape)],
        out_specs=pl.BlockSpec((tb, SUBLANES, LANES), lambda i: (i, 0, 0)),
        out_shape=jax.ShapeDtypeStruct((t, SUBLANES, LANES), F32),
        scratch_shapes=[pltpu.VMEM((tb, P_SEL, LANES), F32)] + _id_scratch(),
        compiler_params=pltpu.CompilerParams(dimension_semantics=("arbitrary",),
                                             vmem_limit_bytes=TABLE_VMEM_LIMIT),
        name="peer_out",
    )(e_flat, w_t, x1_3, gt_groups, table)


def _prep_weights(w_in, b_in, g_qa, g_ka):
    sizes = (A_WIDTH, A_WIDTH, A_WIDTH, A_HEADS, B_WIDTH, B_WIDTH, B_WIDTH, B_HEADS, B_HEADS, B_WIDTH,
             D_MODEL, D_MODEL)
    offs = [0]
    for s in sizes:
        offs.append(offs[-1] + s)
    col = lambda a, i: a[..., offs[i]:offs[i + 1]]
    cat = lambda a, ids: jnp.concatenate([col(a, i) for i in ids], axis=-1)
    wa, ba = cat(w_in, (0, 1, 2)), cat(b_in, (0, 1, 2))
    wb, bb = cat(w_in, (4, 5, 6)), cat(b_in, (4, 5, 6))
    wg, bg = cat(w_in, (9, 10, 11)), cat(b_in, (9, 10, 11))
    ws, bs = cat(w_in, (3, 7, 8)), cat(b_in, (3, 7, 8))
    pad = SMALL_W - ws.shape[-1]
    ws = jnp.pad(ws, ((0, 0), (0, pad)))
    bs = jnp.pad(bs, ((0, pad),))
    wsh, wsl = _split2(ws)
    blk = lax.broadcasted_iota(jnp.int32, (A_WIDTH, A_WIDTH), 0) // A_HEAD_DIM
    blk_t = lax.broadcasted_iota(jnp.int32, (A_WIDTH, A_WIDTH), 1) // A_HEAD_DIM
    bd = jnp.where(blk == blk_t, 1.0 / A_HEAD_DIM, 0.0).astype(BF16)
    r = lambda v: v.reshape(1, -1)
    return (wa.astype(BF16), r(ba), wb.astype(BF16), r(bb), wg.astype(BF16), r(bg), wsh, wsl, r(bs),
            r(g_qa), r(g_ka), bd)


def _gate_rows(small):
    return jnp.swapaxes(small[..., 0:16], 1, 2)


def kernel(x_prompt, x_sample, c_prompt, c_sample, cache_k, cache_v, cache_logf, state_C, state_n, state_m,
           w_mod, b_mod, g_norm1, g_norm2, w_in, b_in, g_qa, g_ka, g_hb, w_pa, w_pb, w_o, w_pq, sub_keys,
           expert_u, expert_v):
    assert w_mod.shape[0] == 1, "single-layer step"
    bp, sp, _ = x_prompt.shape
    bs, ss, _ = x_sample.shape
    past = cache_k.shape[2]
    inw = _prep_weights(w_in[0], b_in[0], g_qa[0], g_ka[0])
    g1, g2 = g_norm1[0].reshape(1, -1), g_norm2[0].reshape(1, -1)
    ghb = g_hb[0].reshape(1, -1)
    post_w = (w_pa[0].astype(BF16), w_pb[0].astype(BF16), w_o[0].astype(BF16), w_pq[0].astype(BF16))
    skeys = sub_keys[0].reshape(2 * P_HEADS, P_KEYS, P_HALF).astype(BF16)
    tab_u, tab_v = _pack_table(expert_u[0]), _pack_table(expert_v[0])

    mod_p = _modulation(c_prompt, w_mod[0], b_mod[0])[:, None, :]
    mod_s = _modulation(c_sample, w_mod[0], b_mod[0])[:, None, :]

    tm = 256
    (qa, ka, va, qb, kb, vb, ob, sga, sgb, small) = _inproj(x_prompt, mod_p, g1, inw, tm)
    srow = _gate_rows(small)
    fk = _cumsum_lanes(srow[:, 0:A_HEADS, :], 512).reshape(bp, A_HEADS // 2, 2, sp)
    ha = _fox_attention(qa, ka, va, fk, tq=min(1024, sp), tk=min(1024, sp), qoff=0)
    zc = jnp.zeros((bp, B_HEADS, B_DIM, B_DIM), F32)
    zn = jnp.zeros((bp, B_HEADS, 1, B_DIM), F32)
    hb, c_p, n_p, m_p = _mlstm(qb, kb, vb, small, srow, zc, zn, zn, chunk=128)
    x1_p, h2_p, pq_p = _postmix(ha, hb, ob, sga, sgb, x_prompt, mod_p, ghb, g2, *post_w, tm)
    state_p = (ka.reshape(1, bp, sp, A_HEADS, A_HEAD_DIM), va.reshape(1, bp, sp, A_HEADS, A_HEAD_DIM),
               small[None, :, :, 0:A_HEADS], c_p[None], n_p.reshape(1, bp, B_HEADS, B_DIM),
               m_p[None, :, :, 0, 0])

    (qa, ka, va, qb, kb, vb, ob, sga, sgb, small) = _inproj(x_sample, mod_s, g1, inw, ss)
    s_all = past + ss
    s_pad = -(-s_all // LANES) * LANES
    padk = lambda a: jnp.pad(a, ((0, 0), (0, s_pad - s_all), (0, 0)))
    kk = padk(jnp.concatenate([cache_k[0].reshape(bs, past, A_WIDTH), ka], axis=1))
    vv = padk(jnp.concatenate([cache_v[0].reshape(bs, past, A_WIDTH), va], axis=1))
    lf_all = padk(jnp.concatenate([cache_logf[0], small[:, :, 0:A_HEADS]], axis=1))
    tc = max(t for t in (512, 384, 256, 128) if s_pad % t == 0)
    fk = _cumsum_lanes(jnp.swapaxes(lf_all, 1, 2), tc).reshape(bs, A_HEADS // 2, 2, s_pad)
    ha = _fox_attention(qa, kk, vv, fk, tq=ss, tk=s_pad, qoff=past)
    chunk = 128
    padt = lambda a: jnp.pad(a, ((0, 0), (0, chunk - ss), (0, 0)))
    colid = jnp.arange(SMALL_W)
    is_input = (colid >= A_HEADS) & (colid < A_HEADS + B_HEADS)
    real = (jnp.arange(chunk) < ss)[None, :, None]
    small_pad = jnp.where(real, padt(small), jnp.where(is_input, NEG, 0.0)[None, None, :])
    c0 = state_C[0].astype(F32)
    n0 = state_n[0].astype(F32).reshape(bs, B_HEADS, 1, B_DIM)
    m0 = jnp.broadcast_to(state_m[0].astype(F32)[:, :, None, None], (bs, B_HEADS, 1, LANES))
    hb, c_s, n_s, m_s = _mlstm(padt(qb), padt(kb), padt(vb), small_pad, _gate_rows(small_pad), c0, n0, m0,
                               chunk=chunk)
    x1_s, h2_s, pq_s = _postmix(ha, hb[:, 0:ss], ob, sga, sgb, x_sample, mod_s, ghb, g2, *post_w, ss)
    state_s = (ka.reshape(1, bs, ss, A_HEADS, A_HEAD_DIM), va.reshape(1, bs, ss, A_HEADS, A_HEAD_DIM),
               small[None, :, :, 0:A_HEADS], c_s[None], n_s.reshape(1, bs, B_HEADS, B_DIM),
               m_s[None, :, :, 0, 0])

    tp, ts = bp * sp, bs * ss
    tt = LANES
    assert tp % tt == 0 and ts % min(tt, ts) == 0
    idx_p, g_p = _peer_topk(pq_p.reshape(tp, D_MODEL), skeys, tt)
    idx_s, g_s = _peer_topk(pq_s.reshape(ts, D_MODEL), skeys, min(tt, ts))
    idx_t = jnp.concatenate([idx_p, idx_s], axis=1)
    g_t = jnp.concatenate([g_p, g_s], axis=1)
    e_flat = jnp.swapaxes(idx_t, 0, 1).reshape(-1)
    tok3 = lambda a, b: jnp.concatenate([a.reshape(tp, SUBLANES, LANES), b.reshape(ts, SUBLANES, LANES)], axis=0)
    x3 = tok3(h2_p, h2_s)
    x1_3 = tok3(x1_p, x1_s)
    tb = LANES
    assert (tp + ts) % tb == 0 and sp % ss == 0 and tb % ss == 0
    w_t = _peer_act(e_flat, x3, g_t, tab_u, tb)
    group = ss
    gt2 = lambda mod, rep: jnp.repeat(mod[:, 0, 5 * D_MODEL:6 * D_MODEL], rep, axis=0)
    gt_groups = jnp.concatenate([gt2(mod_p, sp // group), gt2(mod_s, 1)], axis=0).reshape(-1, SUBLANES, LANES)
    y3 = _peer_out(e_flat, w_t, x1_3, gt_groups, tab_v, tb, group)
    y_p = y3[0:tp].reshape(bp, sp, D_MODEL)
    y_s = y3[tp:].reshape(bs, ss, D_MODEL)
    return (y_p, y_s) + state_p + state_s
```

```python
import functools

import jax
import jax.numpy as jnp
from jax import lax
from jax.experimental import pallas as pl
from jax.experimental.pallas import tpu as pltpu

EPS = 1e-6
NEG = -1e30
D_MODEL = 1024
A_HEADS = 8
A_HEAD_DIM = 64
A_WIDTH = A_HEADS * A_HEAD_DIM
B_HEADS = 4
B_DIM = 128
B_WIDTH = B_HEADS * B_DIM
P_HEADS = 8
P_KEYS = 128
P_HALF = 64
P_TOPK = 16
P_SEL = P_HEADS * P_TOPK
LANES = 128
SUBLANES = 8
SMALL_W = 128
VMEM_LIMIT = 48 * 2**20
TABLE_VMEM_LIMIT = 56 * 2**20
F32 = jnp.float32
BF16 = jnp.bfloat16


def _dot(a, b):
    return jnp.dot(a, b, preferred_element_type=F32)


def _dot_nt(a, b):
    return lax.dot_general(a, b, (((1,), (1,)), ((), ())), preferred_element_type=F32)


def _dot_tn(a, b):
    return lax.dot_general(a, b, (((0,), (0,)), ((), ())), preferred_element_type=F32)


def _dot_f32(a, b):
    return jnp.dot(a, b, preferred_element_type=F32, precision=lax.Precision.HIGHEST)


def _split2(x):
    hi = x.astype(BF16)
    lo = (x - hi.astype(F32)).astype(BF16)
    return hi, lo


def _log_sigmoid(x):
    return jnp.minimum(x, 0.0) - jnp.log(1.0 + jnp.exp(-jnp.abs(x)))


def _sigmoid(x):
    return 1.0 / (1.0 + jnp.exp(-x))


def _resident(shape):
    nd = len(shape)
    return pl.BlockSpec(shape, lambda *_: (0,) * nd, pipeline_mode=pl.Buffered(1))


def _mod_kernel(c_ref, w_ref, b_ref, o_ref):
    c = c_ref[...]
    s = c * _sigmoid(c)
    o_ref[...] = _dot_f32(s, w_ref[...]) + b_ref[...]


def _modulation(c, w_mod, b_mod):
    bsz = c.shape[0]
    n = w_mod.shape[1]
    tn = 1024
    return pl.pallas_call(
        _mod_kernel,
        grid=(n // tn,),
        in_specs=[pl.BlockSpec((bsz, D_MODEL), lambda j: (0, 0)),
                  pl.BlockSpec((D_MODEL, tn), lambda j: (0, j)),
                  pl.BlockSpec((1, tn), lambda j: (0, j))],
        out_specs=pl.BlockSpec((bsz, tn), lambda j: (0, j)),
        out_shape=jax.ShapeDtypeStruct((bsz, n), F32),
        compiler_params=pltpu.CompilerParams(dimension_semantics=("arbitrary",), vmem_limit_bytes=VMEM_LIMIT),
        name="modulation",
    )(c, w_mod, b_mod.reshape(1, n))


def _inproj_kernel(x_ref, mod_ref, g1_ref, wa_ref, ba_ref, wb_ref, bb_ref, wg_ref, bg_ref,
                   wsh_ref, wsl_ref, bs_ref, gq_ref, gk_ref, bd_ref,
                   qa_ref, ka_ref, va_ref, qb_ref, kb_ref, vb_ref, ob_ref, sga_ref, sgb_ref, small_ref):
    x = x_ref[0]
    mod = mod_ref[0]
    sh1 = mod[:, 0:D_MODEL]
    sc1 = mod[:, D_MODEL:2 * D_MODEL]
    h = x * lax.rsqrt(jnp.mean(x * x, axis=-1, keepdims=True) + EPS)
    h = h * g1_ref[...] * (1.0 + sc1) + sh1
    hh, hl = _split2(h)

    za = _dot(hh, wa_ref[...]) + ba_ref[...]
    bd = bd_ref[...]

    def head_norm(z):
        msq = _dot((z * z).astype(BF16), bd)
        return z * lax.rsqrt(msq + EPS)

    q = head_norm(za[:, 0:A_WIDTH]) * gq_ref[...]
    k = head_norm(za[:, A_WIDTH:2 * A_WIDTH]) * gk_ref[...]
    qa_ref[0] = (q * (LOG2E * A_HEAD_DIM ** -0.5)).astype(BF16)
    ka_ref[0] = k
    va_ref[0] = za[:, 2 * A_WIDTH:3 * A_WIDTH]

    zb = _dot(hh, wb_ref[...]) + bb_ref[...]
    qb_ref[0] = zb[:, 0:B_WIDTH].astype(BF16)
    kb_ref[0] = (zb[:, B_WIDTH:2 * B_WIDTH] * (B_DIM ** -0.5)).astype(BF16)
    vb_ref[0] = zb[:, 2 * B_WIDTH:3 * B_WIDTH].astype(BF16)

    zg = _sigmoid(_dot(hh, wg_ref[...]) + bg_ref[...])
    ob_ref[0] = zg[:, 0:B_WIDTH].astype(BF16)
    sga_ref[0] = zg[:, B_WIDTH:B_WIDTH + D_MODEL].astype(BF16)
    sgb_ref[0] = zg[:, B_WIDTH + D_MODEL:B_WIDTH + 2 * D_MODEL].astype(BF16)

    zs = _dot(hh, wsh_ref[...]) + _dot(hl, wsh_ref[...]) + _dot(hh, wsl_ref[...]) + bs_ref[...]
    col = lax.broadcasted_iota(jnp.int32, zs.shape, 1)
    is_input_gate = (col >= A_HEADS) & (col < A_HEADS + B_HEADS)
    small_ref[0] = jnp.where(is_input_gate, zs, _log_sigmoid(zs))


def _inproj(x, mod, g1, wts, tm):
    bsz, seq, _ = x.shape
    (wa, ba, wb, bb, wg, bg, wsh, wsl, bs, gq, gk, bd) = wts
    tok = lambda w, dt: (pl.BlockSpec((1, tm, w), lambda b, i: (b, i, 0)), jax.ShapeDtypeStruct((bsz, seq, w), dt))
    outs = [tok(A_WIDTH, BF16), tok(A_WIDTH, F32), tok(A_WIDTH, F32),
            tok(B_WIDTH, BF16), tok(B_WIDTH, BF16), tok(B_WIDTH, BF16),
            tok(B_WIDTH, BF16), tok(D_MODEL, BF16), tok(D_MODEL, BF16), tok(SMALL_W, F32)]
    consts = [g1, wa, ba, wb, bb, wg, bg, wsh, wsl, bs, gq, gk, bd]
    return pl.pallas_call(
        _inproj_kernel,
        grid=(bsz, seq // tm),
        in_specs=[pl.BlockSpec((1, tm, D_MODEL), lambda b, i: (b, i, 0)),
                  pl.BlockSpec((1, 1, mod.shape[-1]), lambda b, i: (b, 0, 0))]
                 + [_resident(c.shape) for c in consts],
        out_specs=[o[0] for o in outs],
        out_shape=[o[1] for o in outs],
        compiler_params=pltpu.CompilerParams(dimension_semantics=("arbitrary", "arbitrary"),
                                             vmem_limit_bytes=VMEM_LIMIT),
        name="inproj",
    )(x, mod, *consts)


def _cumsum_kernel(x_ref, u_ref, o_ref):
    seq = x_ref.shape[-1]
    offset = jnp.zeros((x_ref.shape[1], 1), F32)
    for c in range(seq // LANES):
        cols = slice(c * LANES, (c + 1) * LANES)
        local = _dot_f32(x_ref[0, :, cols], u_ref[...])
        o_ref[0, :, cols] = local + offset
        offset = offset + local[:, LANES - 1:LANES]


def _cumsum_lanes(x):
    bsz, rows, seq = x.shape
    assert seq % LANES == 0
    upper = (lax.broadcasted_iota(jnp.int32, (LANES, LANES), 0)
             <= lax.broadcasted_iota(jnp.int32, (LANES, LANES), 1)).astype(F32)
    return pl.pallas_call(
        _cumsum_kernel,
        grid=(bsz,),
        in_specs=[pl.BlockSpec((1, rows, seq), lambda b: (b, 0, 0)), _resident((LANES, LANES))],
        out_specs=pl.BlockSpec((1, rows, seq), lambda b: (b, 0, 0)),
        out_shape=jax.ShapeDtypeStruct((bsz, rows, seq), F32),
        compiler_params=pltpu.CompilerParams(dimension_semantics=("arbitrary",), vmem_limit_bytes=VMEM_LIMIT),
        name="cumsum",
    )(x, upper)


LOG2E = 1.4426950408889634


def _fox_kernel(qt_ref, kt_ref, q_ref, k_ref, v_ref, f_ref, o_ref, m_ref, l_ref, acc_ref, *, tq, tk, qoff):
    step_id = pl.program_id(2)
    qi = qt_ref[step_id]
    ki = kt_ref[step_id]
    q_lo = qoff + qi * tq
    last = (q_lo + tq - 1) // tk

    @pl.when(ki == 0)
    def _():
        m_ref[...] = jnp.full_like(m_ref, NEG)
        l_ref[...] = jnp.zeros_like(l_ref)
        acc_ref[...] = jnp.zeros_like(acc_ref)

    def step(masked):
        q = q_ref[0]
        k = k_ref[0].astype(BF16)
        v = v_ref[0].astype(BF16)
        lane = lax.broadcasted_iota(jnp.int32, (tq, LANES), 1)
        first_head = lane < A_HEAD_DIM
        alphas, pvs = [], []
        for h in range(2):
            qh = jnp.where(first_head if h == 0 else ~first_head, q, jnp.zeros_like(q))
            s = _dot_nt(qh, k) - f_ref[0, 0, h:h + 1, :] * LOG2E
            if masked:
                kpos = ki * tk + lax.broadcasted_iota(jnp.int32, (tq, tk), 1)
                qpos = q_lo + lax.broadcasted_iota(jnp.int32, (tq, tk), 0)
                s = jnp.where(kpos <= qpos, s, NEG)
            m_prev = m_ref[h]
            m_new = jnp.maximum(m_prev, jnp.max(s, axis=-1, keepdims=True))
            alpha = jnp.exp2(m_prev - m_new)
            p = jnp.exp2(s - jnp.concatenate([m_new] * (tk // LANES), axis=1))
            l_ref[h] = alpha * l_ref[h] + jnp.sum(p, axis=-1, keepdims=True)
            m_ref[h] = m_new
            alphas.append(alpha)
            pvs.append(_dot(p.astype(BF16), v))
        acc_ref[...] = (acc_ref[...] * jnp.where(first_head, alphas[0], alphas[1])
                        + jnp.where(first_head, pvs[0], pvs[1]))

    needs_mask = (ki + 1) * tk - 1 > q_lo

    @pl.when(needs_mask)
    def _():
        step(True)

    @pl.when(jnp.logical_not(needs_mask))
    def _():
        step(False)

    @pl.when(ki == last)
    def _():
        lane = lax.broadcasted_iota(jnp.int32, (tq, LANES), 1)
        inv = jnp.where(lane < A_HEAD_DIM, 1.0 / l_ref[0], 1.0 / l_ref[1])
        o_ref[0] = (acc_ref[...] * inv).astype(o_ref.dtype)


def _fox_attention(q, k, v, fk, *, tq, tk, qoff):
    bsz, lq, _ = q.shape
    nq = lq // tq
    pairs = A_HEADS // 2
    sched = [(qi, ki) for qi in range(nq) for ki in range((qoff + qi * tq + tq - 1) // tk + 1)]
    qt = jnp.asarray([s[0] for s in sched], jnp.int32)
    kt = jnp.asarray([s[1] for s in sched], jnp.int32)
    grid_spec = pltpu.PrefetchScalarGridSpec(
        num_scalar_prefetch=2,
        grid=(bsz, pairs, len(sched)),
        in_specs=[pl.BlockSpec((1, tq, LANES), lambda b, hp, s, qt, kt: (b, qt[s], hp)),
                  pl.BlockSpec((1, tk, LANES), lambda b, hp, s, qt, kt: (b, kt[s], hp)),
                  pl.BlockSpec((1, tk, LANES), lambda b, hp, s, qt, kt: (b, kt[s], hp)),
                  pl.BlockSpec((1, 1, 2, tk), lambda b, hp, s, qt, kt: (b, hp, 0, kt[s]))],
        out_specs=pl.BlockSpec((1, tq, LANES), lambda b, hp, s, qt, kt: (b, qt[s], hp)),
        scratch_shapes=[pltpu.VMEM((2, tq, LANES), F32), pltpu.VMEM((2, tq, LANES), F32),
                        pltpu.VMEM((tq, LANES), F32)])
    return pl.pallas_call(
        functools.partial(_fox_kernel, tq=tq, tk=tk, qoff=qoff),
        grid_spec=grid_spec,
        out_shape=jax.ShapeDtypeStruct((bsz, lq, A_WIDTH), BF16),
        compiler_params=pltpu.CompilerParams(
            dimension_semantics=("arbitrary", "arbitrary", "arbitrary"), vmem_limit_bytes=VMEM_LIMIT),
        name="fox_attention",
    )(qt, kt, q, k, v, fk)


def _mlstm_kernel(q_ref, k_ref, v_ref, sm_ref, smt_ref, ltri_ref, utri_ref, c0_ref, n0_ref, m0_ref,
                  h_ref, c_ref, n_ref, m_ref, cs_ref, ns_ref, ms_ref, *, chunk):
    ci = pl.program_id(1)

    @pl.when(ci == 0)
    def _():
        cs_ref[...] = c0_ref[0]
        ns_ref[...] = n0_ref[0]
        ms_ref[...] = m0_ref[0]

    sm = sm_ref[0]
    smt = smt_ref[0]
    bcol_all = _dot_f32(ltri_ref[...], sm)
    brow_all = _dot_f32(smt, utri_ref[...])
    row = lax.broadcasted_iota(jnp.int32, (chunk, chunk), 0)
    colx = lax.broadcasted_iota(jnp.int32, (chunk, chunk), 1)
    causal = colx <= row
    i0 = A_HEADS
    f0 = A_HEADS + B_HEADS
    for h in range(B_HEADS):
        sl = slice(h * B_DIM, (h + 1) * B_DIM)
        qh, kh, vh = q_ref[0, :, sl], k_ref[0, :, sl], v_ref[0, :, sl]
        bcol = bcol_all[:, f0 + h:f0 + h + 1]
        brow = brow_all[f0 + h:f0 + h + 1, :]
        irow = smt[i0 + h:i0 + h + 1, :]
        icol = sm[:, i0 + h:i0 + h + 1]
        m0 = ms_ref[h][:, 0:1]
        dmat = jnp.where(causal, bcol - brow + irow, NEG)
        inter = m0 + bcol
        m = jnp.maximum(jnp.max(dmat, axis=-1, keepdims=True), inter)
        w = jnp.exp(dmat - m)
        a = jnp.exp(inter - m)
        ws = w * _dot_nt(qh, kh)
        c_prev = cs_ref[h]
        n_prev = ns_ref[h]
        num = _dot(ws.astype(BF16), vh) + a * _dot(qh, c_prev.astype(BF16))
        den = jnp.sum(ws, axis=-1, keepdims=True) + a * jnp.sum(qh.astype(F32) * n_prev, axis=-1, keepdims=True)
        hh = num / jnp.maximum(jnp.abs(den), jnp.exp(-m))
        hn = hh * lax.rsqrt(jnp.mean(hh * hh, axis=-1, keepdims=True) + EPS)
        h_ref[0, :, sl] = hn.astype(h_ref.dtype)
        m_last = m[chunk - 1:chunk, :]
        b_last = bcol[chunk - 1:chunk, :]
        w_last = jnp.exp(b_last - bcol + icol - m_last)
        a_last = jnp.exp(m0 + b_last - m_last)
        kw = kh.astype(F32) * w_last
        cs_ref[h] = a_last * c_prev + _dot_tn(kw.astype(BF16), vh)
        ns_ref[h] = a_last * n_prev + jnp.sum(kw, axis=0, keepdims=True)
        ms_ref[h] = jnp.broadcast_to(m_last, (1, LANES))

    @pl.when(ci == pl.num_programs(1) - 1)
    def _():
        c_ref[0] = cs_ref[...]
        n_ref[0] = ns_ref[...]
        m_ref[0] = ms_ref[...]


def _mlstm(q, k, v, small, small_t, c0, n0, m0, *, chunk):
    bsz, seq, _ = q.shape
    nc = seq // chunk
    r = lax.broadcasted_iota(jnp.int32, (chunk, chunk), 0)
    c = lax.broadcasted_iota(jnp.int32, (chunk, chunk), 1)
    ltri = (c <= r).astype(F32)
    utri = (r <= c).astype(F32)
    tokb = pl.BlockSpec((1, chunk, B_WIDTH), lambda b, i: (b, i, 0))
    st = lambda shp: pl.BlockSpec((1,) + shp, lambda b, i: (b,) + (0,) * len(shp))
    return pl.pallas_call(
        functools.partial(_mlstm_kernel, chunk=chunk),
        grid=(bsz, nc),
        in_specs=[tokb, tokb, tokb,
                  pl.BlockSpec((1, chunk, SMALL_W), lambda b, i: (b, i, 0)),
                  pl.BlockSpec((1, 16, chunk), lambda b, i: (b, 0, i)),
                  _resident((chunk, chunk)), _resident((chunk, chunk)),
                  st((B_HEADS, B_DIM, B_DIM)), st((B_HEADS, 1, B_DIM)), st((B_HEADS, 1, LANES))],
        out_specs=[tokb, st((B_HEADS, B_DIM, B_DIM)), st((B_HEADS, 1, B_DIM)), st((B_HEADS, 1, LANES))],
        out_shape=[jax.ShapeDtypeStruct((bsz, seq, B_WIDTH), BF16),
                   jax.ShapeDtypeStruct((bsz, B_HEADS, B_DIM, B_DIM), F32),
                   jax.ShapeDtypeStruct((bsz, B_HEADS, 1, B_DIM), F32),
                   jax.ShapeDtypeStruct((bsz, B_HEADS, 1, LANES), F32)],
        scratch_shapes=[pltpu.VMEM((B_HEADS, B_DIM, B_DIM), F32), pltpu.VMEM((B_HEADS, 1, B_DIM), F32),
                        pltpu.VMEM((B_HEADS, 1, LANES), F32)],
        compiler_params=pltpu.CompilerParams(dimension_semantics=("arbitrary", "arbitrary"),
                                             vmem_limit_bytes=VMEM_LIMIT),
        name="mlstm",
    )(q, k, v, small, small_t, ltri, utri, c0, n0, m0)


def _postmix_kernel(ha_ref, hb_ref, ob_ref, sga_ref, sgb_ref, x_ref, mod_ref, ghb_ref, g2_ref,
                    wpa_ref, wpb_ref, wo_ref, wpq_ref, x1_ref, h2_ref, pq_ref):
    mod = mod_ref[0]
    gt1 = mod[:, 2 * D_MODEL:3 * D_MODEL]
    sh2 = mod[:, 3 * D_MODEL:4 * D_MODEL]
    sc2 = mod[:, 4 * D_MODEL:5 * D_MODEL]
    hb = (hb_ref[0].astype(F32) * ghb_ref[...] * ob_ref[0].astype(F32)).astype(BF16)
    merged = (sga_ref[0].astype(F32) * _dot(ha_ref[0], wpa_ref[...])
              + sgb_ref[0].astype(F32) * _dot(hb, wpb_ref[...]))
    x1 = x_ref[0] + gt1 * _dot(merged.astype(BF16), wo_ref[...])
    x1_ref[0] = x1
    h2 = x1 * lax.rsqrt(jnp.mean(x1 * x1, axis=-1, keepdims=True) + EPS)
    h2 = h2 * g2_ref[...] * (1.0 + sc2) + sh2
    tm = h2.shape[0]
    for s in range(SUBLANES):
        h2_ref[0, pl.ds(s, tm, stride=SUBLANES), :] = h2[:, s * LANES:(s + 1) * LANES]
    pq_ref[0] = _dot(h2.astype(BF16), wpq_ref[...]).astype(BF16)


def _postmix(ha, hb, ob, sga, sgb, x, mod, ghb, g2, wpa, wpb, wo, wpq, tm):
    bsz, seq, _ = x.shape
    tok = lambda w: pl.BlockSpec((1, tm, w), lambda b, i: (b, i, 0))
    consts = [ghb, g2, wpa, wpb, wo, wpq]
    return pl.pallas_call(
        _postmix_kernel,
        grid=(bsz, seq // tm),
        in_specs=[tok(A_WIDTH), tok(B_WIDTH), tok(B_WIDTH), tok(D_MODEL), tok(D_MODEL), tok(D_MODEL),
                  pl.BlockSpec((1, 1, mod.shape[-1]), lambda b, i: (b, 0, 0))]
                 + [_resident(c.shape) for c in consts],
        out_specs=[tok(D_MODEL), pl.BlockSpec((1, tm * SUBLANES, LANES), lambda b, i: (b, i, 0)), tok(D_MODEL)],
        out_shape=[jax.ShapeDtypeStruct((bsz, seq, D_MODEL), F32),
                   jax.ShapeDtypeStruct((bsz, seq * SUBLANES, LANES), F32),
                   jax.ShapeDtypeStruct((bsz, seq, D_MODEL), BF16)],
        compiler_params=pltpu.CompilerParams(dimension_semantics=("arbitrary", "arbitrary"),
                                             vmem_limit_bytes=VMEM_LIMIT),
        name="postmix",
    )(ha, hb, ob, sga, sgb, x, mod, *consts)


def _top16(x, ids):
    big = jnp.int32(2**30)
    vals, idxs = [], []
    for _ in range(P_TOPK):
        mx = jnp.max(x, axis=0, keepdims=True)
        am = jnp.min(jnp.where(x == mx, ids, big), axis=0, keepdims=True)
        vals.append(mx)
        idxs.append(am)
        x = jnp.where(ids == am, -jnp.inf, x)
    return jnp.concatenate(vals, axis=0), jnp.concatenate(idxs, axis=0)


def _pair_candidates(a, b):
    t = a.shape[1]
    row16 = lax.broadcasted_iota(jnp.int32, (P_TOPK, t), 0)
    row8 = lax.broadcasted_iota(jnp.int32, (SUBLANES, t), 0)
    vals = [a[0:1, :] + b]
    ids = [row16]
    for i in range(1, SUBLANES):
        vals.append(a[i:i + 1, :] + b[0:SUBLANES, :])
        ids.append(row8 + P_TOPK * i)
    vals.append(a[SUBLANES:, :] + b[0:1, :])
    ids.append((row8 + SUBLANES) * P_TOPK)
    return jnp.concatenate(vals, axis=0), jnp.concatenate(ids, axis=0)


def _pick(table, sel):
    out = jnp.zeros_like(table)
    for a in range(P_TOPK):
        out = jnp.where(sel == a, table[a:a + 1, :], out)
    return out


def _topk_kernel(pq_ref, sk_ref, idx_ref, g_ref):
    pq = pq_ref[...]
    key_ids = lax.broadcasted_iota(jnp.int32, (P_KEYS, pq.shape[0]), 0)
    for h in range(P_HEADS):
        sv, si = [], []
        for p in range(2):
            j = 2 * h + p
            s = _dot_nt(sk_ref[j], pq[:, j * P_HALF:(j + 1) * P_HALF])
            v, i = _top16(s, key_ids)
            sv.append(v)
            si.append(i)
        fv, fi = _top16(*_pair_candidates(sv[0], sv[1]))
        e = _pick(si[0], fi >> 4) * P_KEYS + _pick(si[1], fi & (P_TOPK - 1))
        ex = jnp.exp(fv - fv[0:1, :])
        g = ex / jnp.sum(ex, axis=0, keepdims=True)
        idx_ref[h * P_TOPK:(h + 1) * P_TOPK, :] = e
        g_ref[h * P_TOPK:(h + 1) * P_TOPK, :] = g


def _peer_topk(pq, sub_keys, tt):
    t = pq.shape[0]
    return pl.pallas_call(
        _topk_kernel,
        grid=(t // tt,),
        in_specs=[pl.BlockSpec((tt, D_MODEL), lambda i: (i, 0)), _resident(sub_keys.shape)],
        out_specs=[pl.BlockSpec((P_SEL, tt), lambda i: (0, i)), pl.BlockSpec((P_SEL, tt), lambda i: (0, i))],
        out_shape=[jax.ShapeDtypeStruct((P_SEL, t), jnp.int32), jax.ShapeDtypeStruct((P_SEL, t), F32)],
        compiler_params=pltpu.CompilerParams(dimension_semantics=("arbitrary",), vmem_limit_bytes=VMEM_LIMIT),
        name="peer_topk",
    )(pq, sub_keys)


PEER_SUB = 16


def _pack_table(t):
    e, d = t.shape
    return t.astype(BF16).reshape(e, d // LANES, LANES)


def _merge_sublanes(a, b, step):
    sub = lax.broadcasted_iota(jnp.int32, (SUBLANES, LANES), 0)
    low = (sub & step) == 0
    if 2 * step == SUBLANES:
        return jnp.where(low, a, b) + pltpu.roll(jnp.where(low, b, a), step, axis=0)
    return jnp.where(low, a + pltpu.roll(a, SUBLANES - step, axis=0), b + pltpu.roll(b, step, axis=0))


def _fold8(parts):
    quads = [_merge_sublanes(parts[i], parts[i + 4], 4) for i in range(4)]
    even = _merge_sublanes(quads[0], quads[2], 2)
    odd = _merge_sublanes(quads[1], quads[3], 2)
    return _merge_sublanes(even, odd, 1)


def _id_pipeline(e_hbm, ebuf, sem, body, carry):
    step = pl.program_id(0)
    words = PEER_SUB * P_SEL
    nsub = LANES // PEER_SUB
    total = pl.num_programs(0) * nsub

    def copy(sub, slot):
        return pltpu.make_async_copy(e_hbm.at[pl.ds(sub * words, words)], ebuf.at[slot], sem.at[slot])

    @pl.when(step == 0)
    def _():
        copy(0, 0).start()

    def pair(j, carry):
        sub = step * nsub + 2 * j
        copy(sub, 0).wait()
        copy(sub + 1, 1).start()
        carry = body(0, 2 * j * PEER_SUB, carry)

        @pl.when(sub + 2 < total)
        def _():
            copy(sub + 2, 0).start()
        copy(sub + 1, 1).wait()
        return body(1, (2 * j + 1) * PEER_SUB, carry)
    return lax.fori_loop(0, nsub // 2, pair, carry)


def _peer_act_kernel(e_hbm, x_ref, g_ref, tab_ref, w_ref, q_ref, ebuf, sem, *, tb):
    def gather(slot, tok0, carry):
        for k in range(PEER_SUB):
            x = x_ref[tok0 + k]
            for grp in range(P_SEL // SUBLANES):
                parts = [tab_ref[ebuf[slot, k * P_SEL + grp * SUBLANES + j]].astype(F32) * x
                         for j in range(SUBLANES)]
                q_ref[tok0 + k, grp * SUBLANES:(grp + 1) * SUBLANES, :] = _fold8(parts)
        return carry
    _id_pipeline(e_hbm, ebuf, sem, gather, 0)

    kk = lax.broadcasted_iota(jnp.int32, (2 * LANES, 2 * LANES), 0)
    nn = lax.broadcasted_iota(jnp.int32, (2 * LANES, 2 * LANES), 1)
    ones = jnp.where((kk < LANES) == (nn < LANES), 1.0, 0.0).astype(BF16)
    lane = lax.broadcasted_iota(jnp.int32, (P_SEL, LANES), 1)
    act = jnp.zeros((P_SEL, tb), F32)
    for t in range(0, tb, 2):
        qh, ql = _split2(jnp.concatenate([q_ref[t], q_ref[t + 1]], axis=1))
        r = _dot(qh, ones) + _dot(ql, ones)
        act = jnp.where(lane == t, r[:, 0:LANES], act)
        act = jnp.where(lane == t + 1, r[:, LANES:], act)
    gelu = 0.5 * act * (1.0 + lax.erf(act * (2.0 ** -0.5)))
    w_ref[...] = g_ref[...] * gelu


def _id_scratch():
    return [pltpu.SMEM((2, PEER_SUB * P_SEL), jnp.int32), pltpu.SemaphoreType.DMA((2,))]


def _peer_act(e_flat, x3, g_t, table, tb):
    t = x3.shape[0]
    assert tb == LANES and t % tb == 0
    return pl.pallas_call(
        functools.partial(_peer_act_kernel, tb=tb),
        grid=(t // tb,),
        in_specs=[pl.BlockSpec(memory_space=pltpu.HBM),
                  pl.BlockSpec((tb, SUBLANES, LANES), lambda i: (i, 0, 0)),
                  pl.BlockSpec((P_SEL, tb), lambda i: (0, i)),
                  _resident(table.shape)],
        out_specs=pl.BlockSpec((P_SEL, tb), lambda i: (0, i)),
        out_shape=jax.ShapeDtypeStruct((P_SEL, t), F32),
        scratch_shapes=[pltpu.VMEM((tb, P_SEL, LANES), F32)] + _id_scratch(),
        compiler_params=pltpu.CompilerParams(dimension_semantics=("arbitrary",),
                                             vmem_limit_bytes=TABLE_VMEM_LIMIT),
        name="peer_act",
    )(e_flat, x3, g_t, table)


def _peer_out_kernel(e_hbm, w_ref, x1_ref, gt_ref, tab_ref, o_ref, wb_ref, wsp_ref, y_ref, ebuf, sem, *, tb):
    wh, wl = _split2(w_ref[...])
    wsp_ref[0] = wh
    wsp_ref[1] = wl
    rows = lax.broadcasted_iota(jnp.int32, (tb, 2 * LANES), 0)
    second = lax.broadcasted_iota(jnp.int32, (tb, 2 * LANES), 1) >= LANES

    def spread(slot, t0, k):
        onehot = jnp.where(rows == jnp.where(second, t0 + k + 1, t0 + k), 1.0, 0.0).astype(BF16)
        both = _dot(wsp_ref[0], onehot) + _dot(wsp_ref[1], onehot)
        wb_ref[slot, k] = both[:, 0:LANES]
        wb_ref[slot, k + 1] = both[:, LANES:]

    for k in range(0, PEER_SUB, 2):
        spread(0, 0, k)

    def gather(slot, tok0, carry):
        for k in range(PEER_SUB):
            acc = [jnp.zeros((SUBLANES, LANES), F32), jnp.zeros((SUBLANES, LANES), F32)]
            for c in range(P_SEL):
                row = tab_ref[ebuf[slot, k * P_SEL + c]].astype(F32)
                w = jnp.broadcast_to(wb_ref[slot, k, c:c + 1, :], (SUBLANES, LANES))
                acc[c % 2] = acc[c % 2] + row * w
            y_ref[pl.ds(pl.multiple_of((tok0 + k) * SUBLANES, SUBLANES), SUBLANES), :] = acc[0] + acc[1]
            if k % 2 == 1:
                spread(1 - slot, tok0 + PEER_SUB, k - 1)
        return carry
    _id_pipeline(e_hbm, ebuf, sem, gather, 0)

    gate = gt_ref[0] if len(gt_ref.shape) == 3 else gt_ref[...]
    for s in range(SUBLANES):
        cols = slice(s * LANES, (s + 1) * LANES)
        o_ref[:, cols] = x1_ref[:, cols] + gate[:, cols] * y_ref[pl.ds(s, tb, stride=SUBLANES), :]


def _peer_out(e_flat, w_t, x1, gt, table, tb):
    t = x1.shape[0]
    g = gt.shape[0]
    assert tb == LANES and t % tb == 0
    if g == t:
        gt_spec = pl.BlockSpec((tb, D_MODEL), lambda i: (i, 0))
    else:
        per = t // g
        assert per % tb == 0
        gt = gt.reshape(g, 1, D_MODEL)
        gt_spec = pl.BlockSpec((1, 1, D_MODEL), lambda i: ((i * tb) // per, 0, 0))
    return pl.pallas_call(
        functools.partial(_peer_out_kernel, tb=tb),
        grid=(t // tb,),
        in_specs=[pl.BlockSpec(memory_space=pltpu.HBM),
                  pl.BlockSpec((P_SEL, tb), lambda i: (0, i)),
                  pl.BlockSpec((tb, D_MODEL), lambda i: (i, 0)),
                  gt_spec,
                  _resident(table.shape)],
        out_specs=pl.BlockSpec((tb, D_MODEL), lambda i: (i, 0)),
        out_shape=jax.ShapeDtypeStruct((t, D_MODEL), F32),
        scratch_shapes=[pltpu.VMEM((2, PEER_SUB, P_SEL, LANES), F32), pltpu.VMEM((2, P_SEL, tb), BF16),
                        pltpu.VMEM((tb * SUBLANES, LANES), F32)] + _id_scratch(),
        compiler_params=pltpu.CompilerParams(dimension_semantics=("arbitrary",),
                                             vmem_limit_bytes=TABLE_VMEM_LIMIT),
        name="peer_out",
    )(e_flat, w_t, x1, gt, table)


def _prep_weights(w_in, b_in, g_qa, g_ka):
    sizes = (A_WIDTH, A_WIDTH, A_WIDTH, A_HEADS, B_WIDTH, B_WIDTH, B_WIDTH, B_HEADS, B_HEADS, B_WIDTH,
             D_MODEL, D_MODEL)
    offs = [0]
    for s in sizes:
        offs.append(offs[-1] + s)
    col = lambda a, i: a[..., offs[i]:offs[i + 1]]
    cat = lambda a, ids: jnp.concatenate([col(a, i) for i in ids], axis=-1)
    wa, ba = cat(w_in, (0, 1, 2)), cat(b_in, (0, 1, 2))
    wb, bb = cat(w_in, (4, 5, 6)), cat(b_in, (4, 5, 6))
    wg, bg = cat(w_in, (9, 10, 11)), cat(b_in, (9, 10, 11))
    ws, bs = cat(w_in, (3, 7, 8)), cat(b_in, (3, 7, 8))
    pad = SMALL_W - ws.shape[-1]
    ws = jnp.pad(ws, ((0, 0), (0, pad)))
    bs = jnp.pad(bs, ((0, pad),))
    wsh, wsl = _split2(ws)
    blk = lax.broadcasted_iota(jnp.int32, (A_WIDTH, A_WIDTH), 0) // A_HEAD_DIM
    blk_t = lax.broadcasted_iota(jnp.int32, (A_WIDTH, A_WIDTH), 1) // A_HEAD_DIM
    bd = jnp.where(blk == blk_t, 1.0 / A_HEAD_DIM, 0.0).astype(BF16)
    r = lambda v: v.reshape(1, -1)
    return (wa.astype(BF16), r(ba), wb.astype(BF16), r(bb), wg.astype(BF16), r(bg), wsh, wsl, r(bs),
            r(g_qa), r(g_ka), bd)


def _gate_rows(small):
    return jnp.swapaxes(small[..., 0:16], 1, 2)


def kernel(x_prompt, x_sample, c_prompt, c_sample, cache_k, cache_v, cache_logf, state_C, state_n, state_m,
           w_mod, b_mod, g_norm1, g_norm2, w_in, b_in, g_qa, g_ka, g_hb, w_pa, w_pb, w_o, w_pq, sub_keys,
           expert_u, expert_v):
    assert w_mod.shape[0] == 1, "single-layer step"
    bp, sp, _ = x_prompt.shape
    bs, ss, _ = x_sample.shape
    past = cache_k.shape[2]
    inw = _prep_weights(w_in[0], b_in[0], g_qa[0], g_ka[0])
    g1, g2 = g_norm1[0].reshape(1, -1), g_norm2[0].reshape(1, -1)
    ghb = g_hb[0].reshape(1, -1)
    post_w = (w_pa[0].astype(BF16), w_pb[0].astype(BF16), w_o[0].astype(BF16), w_pq[0].astype(BF16))
    skeys = sub_keys[0].reshape(2 * P_HEADS, P_KEYS, P_HALF).astype(BF16)
    tab_u, tab_v = _pack_table(expert_u[0]), _pack_table(expert_v[0])

    mod_p = _modulation(c_prompt, w_mod[0], b_mod[0])[:, None, :]
    mod_s = _modulation(c_sample, w_mod[0], b_mod[0])[:, None, :]

    tm = 256
    (qa, ka, va, qb, kb, vb, ob, sga, sgb, small) = _inproj(x_prompt, mod_p, g1, inw, tm)
    srow = _gate_rows(small)
    fk = _cumsum_lanes(srow[:, 0:A_HEADS, :]).reshape(bp, A_HEADS // 2, 2, sp)
    ha = _fox_attention(qa, ka, va, fk, tq=min(1024, sp), tk=min(1024, sp), qoff=0)
    zc = jnp.zeros((bp, B_HEADS, B_DIM, B_DIM), F32)
    zn = jnp.zeros((bp, B_HEADS, 1, B_DIM), F32)
    hb, c_p, n_p, m_p = _mlstm(qb, kb, vb, small, srow, zc, zn, zn, chunk=128)
    x1_p, h2_p, pq_p = _postmix(ha, hb, ob, sga, sgb, x_prompt, mod_p, ghb, g2, *post_w, tm)
    state_p = (ka.reshape(1, bp, sp, A_HEADS, A_HEAD_DIM), va.reshape(1, bp, sp, A_HEADS, A_HEAD_DIM),
               small[None, :, :, 0:A_HEADS], c_p[None], n_p.reshape(1, bp, B_HEADS, B_DIM),
               m_p[None, :, :, 0, 0])

    (qa, ka, va, qb, kb, vb, ob, sga, sgb, small) = _inproj(x_sample, mod_s, g1, inw, ss)
    s_all = past + ss
    s_pad = -(-s_all // LANES) * LANES
    padk = lambda a: jnp.pad(a, ((0, 0), (0, s_pad - s_all), (0, 0)))
    kk = padk(jnp.concatenate([cache_k[0].reshape(bs, past, A_WIDTH), ka], axis=1))
    vv = padk(jnp.concatenate([cache_v[0].reshape(bs, past, A_WIDTH), va], axis=1))
    lf_all = padk(jnp.concatenate([cache_logf[0], small[:, :, 0:A_HEADS]], axis=1))
    fk = _cumsum_lanes(jnp.swapaxes(lf_all, 1, 2)).reshape(bs, A_HEADS // 2, 2, s_pad)
    ha = _fox_attention(qa, kk, vv, fk, tq=ss, tk=s_pad, qoff=past)
    chunk = 128
    padt = lambda a: jnp.pad(a, ((0, 0), (0, chunk - ss), (0, 0)))
    colid = jnp.arange(SMALL_W)
    is_input = (colid >= A_HEADS) & (colid < A_HEADS + B_HEADS)
    real = (jnp.arange(chunk) < ss)[None, :, None]
    small_pad = jnp.where(real, padt(small), jnp.where(is_input, NEG, 0.0)[None, None, :])
    c0 = state_C[0].astype(F32)
    n0 = state_n[0].astype(F32).reshape(bs, B_HEADS, 1, B_DIM)
    m0 = jnp.broadcast_to(state_m[0].astype(F32)[:, :, None, None], (bs, B_HEADS, 1, LANES))
    hb, c_s, n_s, m_s = _mlstm(padt(qb), padt(kb), padt(vb), small_pad, _gate_rows(small_pad), c0, n0, m0,
                               chunk=chunk)
    x1_s, h2_s, pq_s = _postmix(ha, hb[:, 0:ss], ob, sga, sgb, x_sample, mod_s, ghb, g2, *post_w, ss)
    state_s = (ka.reshape(1, bs, ss, A_HEADS, A_HEAD_DIM), va.reshape(1, bs, ss, A_HEADS, A_HEAD_DIM),
               small[None, :, :, 0:A_HEADS], c_s[None], n_s.reshape(1, bs, B_HEADS, B_DIM),
               m_s[None, :, :, 0, 0])

    def channel_mix(x1, h2_tiles, pq, gt):
        t = x1.shape[0] * x1.shape[1]
        tb = LANES
        assert t % tb == 0
        idx_t, g_t = _peer_topk(pq.reshape(t, D_MODEL), skeys, tb)
        e_flat = jnp.swapaxes(idx_t, 0, 1).reshape(-1)
        w_t = _peer_act(e_flat, h2_tiles.reshape(t, SUBLANES, LANES), g_t, tab_u, tb)
        return _peer_out(e_flat, w_t, x1.reshape(t, D_MODEL), gt, tab_v, tb).reshape(x1.shape)

    gt2 = lambda mod, rep: jnp.repeat(mod[:, 0, 5 * D_MODEL:6 * D_MODEL], rep, axis=0)
    tp, ts = bp * sp, bs * ss
    both = lambda a, b, w: jnp.concatenate([a.reshape(-1, w), b.reshape(-1, w)], axis=0)[None]
    y = channel_mix(both(x1_p, x1_s, D_MODEL), both(h2_p, h2_s, LANES), both(pq_p, pq_s, D_MODEL),
                    jnp.concatenate([gt2(mod_p, sp), gt2(mod_s, ss)], axis=0))[0]
    y_p = y[0:tp].reshape(bp, sp, D_MODEL)
    y_s = y[tp:].reshape(bs, ss, D_MODEL)
    return (y_p, y_s) + state_p + state_s
```

```python
import functools

import jax
import jax.numpy as jnp
from jax import lax
from jax.experimental import pallas as pl
from jax.experimental.pallas import tpu as pltpu

EPS = 1e-6
NEG = -1e30
D_MODEL = 1024
A_HEADS = 8
A_HEAD_DIM = 64
A_WIDTH = A_HEADS * A_HEAD_DIM
B_HEADS = 4
B_DIM = 128
B_WIDTH = B_HEADS * B_DIM
P_HEADS = 8
P_KEYS = 128
P_HALF = 64
P_TOPK = 16
P_SEL = P_HEADS * P_TOPK
LANES = 128
SUBLANES = 8
SMALL_W = 128
VMEM_LIMIT = 48 * 2**20
TABLE_VMEM_LIMIT = 56 * 2**20
F32 = jnp.float32
BF16 = jnp.bfloat16


def _dot(a, b):
    return jnp.dot(a, b, preferred_element_type=F32)


def _dot_nt(a, b):
    return lax.dot_general(a, b, (((1,), (1,)), ((), ())), preferred_element_type=F32)


def _dot_tn(a, b):
    return lax.dot_general(a, b, (((0,), (0,)), ((), ())), preferred_element_type=F32)


def _dot_f32(a, b):
    return jnp.dot(a, b, preferred_element_type=F32, precision=lax.Precision.HIGHEST)


def _split2(x):
    hi = x.astype(BF16)
    lo = (x - hi.astype(F32)).astype(BF16)
    return hi, lo


def _log_sigmoid(x):
    return jnp.minimum(x, 0.0) - jnp.log(1.0 + jnp.exp(-jnp.abs(x)))


def _sigmoid(x):
    return 1.0 / (1.0 + jnp.exp(-x))


def _resident(shape):
    nd = len(shape)
    return pl.BlockSpec(shape, lambda *_: (0,) * nd, pipeline_mode=pl.Buffered(1))


def _mod_kernel(c_ref, w_ref, b_ref, o_ref):
    c = c_ref[...]
    s = c * _sigmoid(c)
    o_ref[...] = _dot_f32(s, w_ref[...]) + b_ref[...]


def _modulation(c, w_mod, b_mod):
    bsz = c.shape[0]
    n = w_mod.shape[1]
    tn = 1024
    return pl.pallas_call(
        _mod_kernel,
        grid=(n // tn,),
        in_specs=[pl.BlockSpec((bsz, D_MODEL), lambda j: (0, 0)),
                  pl.BlockSpec((D_MODEL, tn), lambda j: (0, j)),
                  pl.BlockSpec((1, tn), lambda j: (0, j))],
        out_specs=pl.BlockSpec((bsz, tn), lambda j: (0, j)),
        out_shape=jax.ShapeDtypeStruct((bsz, n), F32),
        compiler_params=pltpu.CompilerParams(dimension_semantics=("arbitrary",), vmem_limit_bytes=VMEM_LIMIT),
        name="modulation",
    )(c, w_mod, b_mod.reshape(1, n))


def _inproj_kernel(x_ref, mod_ref, g1_ref, wa_ref, ba_ref, wb_ref, bb_ref, wg_ref, bg_ref,
                   wsh_ref, wsl_ref, bs_ref, gq_ref, gk_ref, bd_ref,
                   qa_ref, ka_ref, va_ref, qb_ref, kb_ref, vb_ref, ob_ref, sga_ref, sgb_ref, small_ref):
    x = x_ref[0]
    mod = mod_ref[0]
    sh1 = mod[:, 0:D_MODEL]
    sc1 = mod[:, D_MODEL:2 * D_MODEL]
    h = x * lax.rsqrt(jnp.mean(x * x, axis=-1, keepdims=True) + EPS)
    h = h * g1_ref[...] * (1.0 + sc1) + sh1
    hh, hl = _split2(h)

    za = _dot(hh, wa_ref[...]) + ba_ref[...]
    bd = bd_ref[...]

    def head_norm(z):
        msq = _dot((z * z).astype(BF16), bd)
        return z * lax.rsqrt(msq + EPS)

    q = head_norm(za[:, 0:A_WIDTH]) * gq_ref[...]
    k = head_norm(za[:, A_WIDTH:2 * A_WIDTH]) * gk_ref[...]
    qa_ref[0] = (q * (LOG2E * A_HEAD_DIM ** -0.5)).astype(BF16)
    ka_ref[0] = k
    va_ref[0] = za[:, 2 * A_WIDTH:3 * A_WIDTH]

    zb = _dot(hh, wb_ref[...]) + bb_ref[...]
    qb_ref[0] = zb[:, 0:B_WIDTH].astype(BF16)
    kb_ref[0] = (zb[:, B_WIDTH:2 * B_WIDTH] * (B_DIM ** -0.5)).astype(BF16)
    vb_ref[0] = zb[:, 2 * B_WIDTH:3 * B_WIDTH].astype(BF16)

    zg = _sigmoid(_dot(hh, wg_ref[...]) + bg_ref[...])
    ob_ref[0] = zg[:, 0:B_WIDTH].astype(BF16)
    sga_ref[0] = zg[:, B_WIDTH:B_WIDTH + D_MODEL].astype(BF16)
    sgb_ref[0] = zg[:, B_WIDTH + D_MODEL:B_WIDTH + 2 * D_MODEL].astype(BF16)

    zs = _dot(hh, wsh_ref[...]) + _dot(hl, wsh_ref[...]) + _dot(hh, wsl_ref[...]) + bs_ref[...]
    col = lax.broadcasted_iota(jnp.int32, zs.shape, 1)
    is_input_gate = (col >= A_HEADS) & (col < A_HEADS + B_HEADS)
    small_ref[0] = jnp.where(is_input_gate, zs, _log_sigmoid(zs))


def _inproj(x, mod, g1, wts, tm):
    bsz, seq, _ = x.shape
    (wa, ba, wb, bb, wg, bg, wsh, wsl, bs, gq, gk, bd) = wts
    tok = lambda w, dt: (pl.BlockSpec((1, tm, w), lambda b, i: (b, i, 0)), jax.ShapeDtypeStruct((bsz, seq, w), dt))
    outs = [tok(A_WIDTH, BF16), tok(A_WIDTH, F32), tok(A_WIDTH, F32),
            tok(B_WIDTH, BF16), tok(B_WIDTH, BF16), tok(B_WIDTH, BF16),
            tok(B_WIDTH, BF16), tok(D_MODEL, BF16), tok(D_MODEL, BF16), tok(SMALL_W, F32)]
    consts = [g1, wa, ba, wb, bb, wg, bg, wsh, wsl, bs, gq, gk, bd]
    return pl.pallas_call(
        _inproj_kernel,
        grid=(bsz, seq // tm),
        in_specs=[pl.BlockSpec((1, tm, D_MODEL), lambda b, i: (b, i, 0)),
                  pl.BlockSpec((1, 1, mod.shape[-1]), lambda b, i: (b, 0, 0))]
                 + [_resident(c.shape) for c in consts],
        out_specs=[o[0] for o in outs],
        out_shape=[o[1] for o in outs],
        compiler_params=pltpu.CompilerParams(dimension_semantics=("arbitrary", "arbitrary"),
                                             vmem_limit_bytes=VMEM_LIMIT),
        name="inproj",
    )(x, mod, *consts)


def _cumsum_kernel(x_ref, u_ref, o_ref):
    seq = x_ref.shape[-1]
    offset = jnp.zeros((x_ref.shape[1], 1), F32)
    for c in range(seq // LANES):
        cols = slice(c * LANES, (c + 1) * LANES)
        local = _dot_f32(x_ref[0, :, cols], u_ref[...])
        o_ref[0, :, cols] = local + offset
        offset = offset + local[:, LANES - 1:LANES]


def _cumsum_lanes(x):
    bsz, rows, seq = x.shape
    assert seq % LANES == 0
    upper = (lax.broadcasted_iota(jnp.int32, (LANES, LANES), 0)
             <= lax.broadcasted_iota(jnp.int32, (LANES, LANES), 1)).astype(F32)
    return pl.pallas_call(
        _cumsum_kernel,
        grid=(bsz,),
        in_specs=[pl.BlockSpec((1, rows, seq), lambda b: (b, 0, 0)), _resident((LANES, LANES))],
        out_specs=pl.BlockSpec((1, rows, seq), lambda b: (b, 0, 0)),
        out_shape=jax.ShapeDtypeStruct((bsz, rows, seq), F32),
        compiler_params=pltpu.CompilerParams(dimension_semantics=("arbitrary",), vmem_limit_bytes=VMEM_LIMIT),
        name="cumsum",
    )(x, upper)


LOG2E = 1.4426950408889634


def _fox_kernel(qt_ref, kt_ref, q_ref, k_ref, v_ref, f_ref, o_ref, m_ref, l_ref, acc_ref, *, tq, tk, qoff):
    step_id = pl.program_id(2)
    qi = qt_ref[step_id]
    ki = kt_ref[step_id]
    q_lo = qoff + qi * tq
    last = (q_lo + tq - 1) // tk

    @pl.when(ki == 0)
    def _():
        m_ref[...] = jnp.full_like(m_ref, NEG)
        l_ref[...] = jnp.zeros_like(l_ref)
        acc_ref[...] = jnp.zeros_like(acc_ref)

    def step(masked):
        q = q_ref[0]
        k = k_ref[0].astype(BF16)
        v = v_ref[0].astype(BF16)
        lane = lax.broadcasted_iota(jnp.int32, (tq, LANES), 1)
        first_head = lane < A_HEAD_DIM
        alphas, pvs = [], []
        for h in range(2):
            qh = jnp.where(first_head if h == 0 else ~first_head, q, jnp.zeros_like(q))
            s = _dot_nt(qh, k) - f_ref[0, 0, h:h + 1, :] * LOG2E
            if masked:
                kpos = ki * tk + lax.broadcasted_iota(jnp.int32, (tq, tk), 1)
                qpos = q_lo + lax.broadcasted_iota(jnp.int32, (tq, tk), 0)
                s = jnp.where(kpos <= qpos, s, NEG)
            m_prev = m_ref[h]
            m_new = jnp.maximum(m_prev, jnp.max(s, axis=-1, keepdims=True))
            alpha = jnp.exp2(m_prev - m_new)
            p = jnp.exp2(s - jnp.concatenate([m_new] * (tk // LANES), axis=1))
            l_ref[h] = alpha * l_ref[h] + jnp.sum(p, axis=-1, keepdims=True)
            m_ref[h] = m_new
            alphas.append(alpha)
            pvs.append(_dot(p.astype(BF16), v))
        acc_ref[...] = (acc_ref[...] * jnp.where(first_head, alphas[0], alphas[1])
                        + jnp.where(first_head, pvs[0], pvs[1]))

    needs_mask = (ki + 1) * tk - 1 > q_lo

    @pl.when(needs_mask)
    def _():
        step(True)

    @pl.when(jnp.logical_not(needs_mask))
    def _():
        step(False)

    @pl.when(ki == last)
    def _():
        lane = lax.broadcasted_iota(jnp.int32, (tq, LANES), 1)
        inv = jnp.where(lane < A_HEAD_DIM, 1.0 / l_ref[0], 1.0 / l_ref[1])
        o_ref[0] = (acc_ref[...] * inv).astype(o_ref.dtype)


def _fox_attention(q, k, v, fk, *, tq, tk, qoff):
    bsz, lq, _ = q.shape
    nq = lq // tq
    pairs = A_HEADS // 2
    sched = [(qi, ki) for qi in range(nq) for ki in range((qoff + qi * tq + tq - 1) // tk + 1)]
    qt = jnp.asarray([s[0] for s in sched], jnp.int32)
    kt = jnp.asarray([s[1] for s in sched], jnp.int32)
    grid_spec = pltpu.PrefetchScalarGridSpec(
        num_scalar_prefetch=2,
        grid=(bsz, pairs, len(sched)),
        in_specs=[pl.BlockSpec((1, tq, LANES), lambda b, hp, s, qt, kt: (b, qt[s], hp)),
                  pl.BlockSpec((1, tk, LANES), lambda b, hp, s, qt, kt: (b, kt[s], hp)),
                  pl.BlockSpec((1, tk, LANES), lambda b, hp, s, qt, kt: (b, kt[s], hp)),
                  pl.BlockSpec((1, 1, 2, tk), lambda b, hp, s, qt, kt: (b, hp, 0, kt[s]))],
        out_specs=pl.BlockSpec((1, tq, LANES), lambda b, hp, s, qt, kt: (b, qt[s], hp)),
        scratch_shapes=[pltpu.VMEM((2, tq, LANES), F32), pltpu.VMEM((2, tq, LANES), F32),
                        pltpu.VMEM((tq, LANES), F32)])
    return pl.pallas_call(
        functools.partial(_fox_kernel, tq=tq, tk=tk, qoff=qoff),
        grid_spec=grid_spec,
        out_shape=jax.ShapeDtypeStruct((bsz, lq, A_WIDTH), BF16),
        compiler_params=pltpu.CompilerParams(
            dimension_semantics=("arbitrary", "arbitrary", "arbitrary"), vmem_limit_bytes=VMEM_LIMIT),
        name="fox_attention",
    )(qt, kt, q, k, v, fk)


def _mlstm_kernel(q_ref, k_ref, v_ref, sm_ref, smt_ref, ltri_ref, utri_ref, c0_ref, n0_ref, m0_ref,
                  h_ref, c_ref, n_ref, m_ref, cs_ref, ns_ref, ms_ref, *, chunk):
    ci = pl.program_id(1)

    @pl.when(ci == 0)
    def _():
        cs_ref[...] = c0_ref[0]
        ns_ref[...] = n0_ref[0]
        ms_ref[...] = m0_ref[0]

    sm = sm_ref[0]
    smt = smt_ref[0]
    bcol_all = _dot_f32(ltri_ref[...], sm)
    brow_all = _dot_f32(smt, utri_ref[...])
    row = lax.broadcasted_iota(jnp.int32, (chunk, chunk), 0)
    colx = lax.broadcasted_iota(jnp.int32, (chunk, chunk), 1)
    causal = colx <= row
    i0 = A_HEADS
    f0 = A_HEADS + B_HEADS
    for h in range(B_HEADS):
        sl = slice(h * B_DIM, (h + 1) * B_DIM)
        qh, kh, vh = q_ref[0, :, sl], k_ref[0, :, sl], v_ref[0, :, sl]
        bcol = bcol_all[:, f0 + h:f0 + h + 1]
        brow = brow_all[f0 + h:f0 + h + 1, :]
        irow = smt[i0 + h:i0 + h + 1, :]
        icol = sm[:, i0 + h:i0 + h + 1]
        m0 = ms_ref[h][:, 0:1]
        dmat = jnp.where(causal, bcol - brow + irow, NEG)
        inter = m0 + bcol
        m = jnp.maximum(jnp.max(dmat, axis=-1, keepdims=True), inter)
        w = jnp.exp(dmat - m)
        a = jnp.exp(inter - m)
        ws = w * _dot_nt(qh, kh)
        c_prev = cs_ref[h]
        n_prev = ns_ref[h]
        num = _dot(ws.astype(BF16), vh) + a * _dot(qh, c_prev.astype(BF16))
        den = jnp.sum(ws, axis=-1, keepdims=True) + a * jnp.sum(qh.astype(F32) * n_prev, axis=-1, keepdims=True)
        hh = num / jnp.maximum(jnp.abs(den), jnp.exp(-m))
        hn = hh * lax.rsqrt(jnp.mean(hh * hh, axis=-1, keepdims=True) + EPS)
        h_ref[0, :, sl] = hn.astype(h_ref.dtype)
        m_last = m[chunk - 1:chunk, :]
        b_last = bcol[chunk - 1:chunk, :]
        w_last = jnp.exp(b_last - bcol + icol - m_last)
        a_last = jnp.exp(m0 + b_last - m_last)
        kw = kh.astype(F32) * w_last
        cs_ref[h] = a_last * c_prev + _dot_tn(kw.astype(BF16), vh)
        ns_ref[h] = a_last * n_prev + jnp.sum(kw, axis=0, keepdims=True)
        ms_ref[h] = jnp.broadcast_to(m_last, (1, LANES))

    @pl.when(ci == pl.num_programs(1) - 1)
    def _():
        c_ref[0] = cs_ref[...]
        n_ref[0] = ns_ref[...]
        m_ref[0] = ms_ref[...]


def _mlstm(q, k, v, small, small_t, c0, n0, m0, *, chunk):
    bsz, seq, _ = q.shape
    nc = seq // chunk
    r = lax.broadcasted_iota(jnp.int32, (chunk, chunk), 0)
    c = lax.broadcasted_iota(jnp.int32, (chunk, chunk), 1)
    ltri = (c <= r).astype(F32)
    utri = (r <= c).astype(F32)
    tokb = pl.BlockSpec((1, chunk, B_WIDTH), lambda b, i: (b, i, 0))
    st = lambda shp: pl.BlockSpec((1,) + shp, lambda b, i: (b,) + (0,) * len(shp))
    return pl.pallas_call(
        functools.partial(_mlstm_kernel, chunk=chunk),
        grid=(bsz, nc),
        in_specs=[tokb, tokb, tokb,
                  pl.BlockSpec((1, chunk, SMALL_W), lambda b, i: (b, i, 0)),
                  pl.BlockSpec((1, 16, chunk), lambda b, i: (b, 0, i)),
                  _resident((chunk, chunk)), _resident((chunk, chunk)),
                  st((B_HEADS, B_DIM, B_DIM)), st((B_HEADS, 1, B_DIM)), st((B_HEADS, 1, LANES))],
        out_specs=[tokb, st((B_HEADS, B_DIM, B_DIM)), st((B_HEADS, 1, B_DIM)), st((B_HEADS, 1, LANES))],
        out_shape=[jax.ShapeDtypeStruct((bsz, seq, B_WIDTH), BF16),
                   jax.ShapeDtypeStruct((bsz, B_HEADS, B_DIM, B_DIM), F32),
                   jax.ShapeDtypeStruct((bsz, B_HEADS, 1, B_DIM), F32),
                   jax.ShapeDtypeStruct((bsz, B_HEADS, 1, LANES), F32)],
        scratch_shapes=[pltpu.VMEM((B_HEADS, B_DIM, B_DIM), F32), pltpu.VMEM((B_HEADS, 1, B_DIM), F32),
                        pltpu.VMEM((B_HEADS, 1, LANES), F32)],
        compiler_params=pltpu.CompilerParams(dimension_semantics=("arbitrary", "arbitrary"),
                                             vmem_limit_bytes=VMEM_LIMIT),
        name="mlstm",
    )(q, k, v, small, small_t, ltri, utri, c0, n0, m0)


def _postmix_kernel(ha_ref, hb_ref, ob_ref, sga_ref, sgb_ref, x_ref, mod_ref, ghb_ref, g2_ref,
                    wpa_ref, wpb_ref, wo_ref, wpq_ref, x1_ref, h2_ref, pq_ref):
    mod = mod_ref[0]
    gt1 = mod[:, 2 * D_MODEL:3 * D_MODEL]
    sh2 = mod[:, 3 * D_MODEL:4 * D_MODEL]
    sc2 = mod[:, 4 * D_MODEL:5 * D_MODEL]
    hb = (hb_ref[0].astype(F32) * ghb_ref[...] * ob_ref[0].astype(F32)).astype(BF16)
    merged = (sga_ref[0].astype(F32) * _dot(ha_ref[0], wpa_ref[...])
              + sgb_ref[0].astype(F32) * _dot(hb, wpb_ref[...]))
    x1 = x_ref[0] + gt1 * _dot(merged.astype(BF16), wo_ref[...])
    x1_ref[0] = x1
    h2 = x1 * lax.rsqrt(jnp.mean(x1 * x1, axis=-1, keepdims=True) + EPS)
    h2 = h2 * g2_ref[...] * (1.0 + sc2) + sh2
    tm = h2.shape[0]
    for s in range(SUBLANES):
        h2_ref[0, pl.ds(s, tm, stride=SUBLANES), :] = h2[:, s * LANES:(s + 1) * LANES]
    pq_ref[0] = _dot(h2.astype(BF16), wpq_ref[...]).astype(BF16)


def _postmix(ha, hb, ob, sga, sgb, x, mod, ghb, g2, wpa, wpb, wo, wpq, tm):
    bsz, seq, _ = x.shape
    tok = lambda w: pl.BlockSpec((1, tm, w), lambda b, i: (b, i, 0))
    consts = [ghb, g2, wpa, wpb, wo, wpq]
    return pl.pallas_call(
        _postmix_kernel,
        grid=(bsz, seq // tm),
        in_specs=[tok(A_WIDTH), tok(B_WIDTH), tok(B_WIDTH), tok(D_MODEL), tok(D_MODEL), tok(D_MODEL),
                  pl.BlockSpec((1, 1, mod.shape[-1]), lambda b, i: (b, 0, 0))]
                 + [_resident(c.shape) for c in consts],
        out_specs=[tok(D_MODEL), pl.BlockSpec((1, tm * SUBLANES, LANES), lambda b, i: (b, i, 0)), tok(D_MODEL)],
        out_shape=[jax.ShapeDtypeStruct((bsz, seq, D_MODEL), F32),
                   jax.ShapeDtypeStruct((bsz, seq * SUBLANES, LANES), F32),
                   jax.ShapeDtypeStruct((bsz, seq, D_MODEL), BF16)],
        compiler_params=pltpu.CompilerParams(dimension_semantics=("arbitrary", "arbitrary"),
                                             vmem_limit_bytes=VMEM_LIMIT),
        name="postmix",
    )(ha, hb, ob, sga, sgb, x, mod, *consts)


def _top16(x, ids):
    big = jnp.int32(2**30)
    vals, idxs = [], []
    for _ in range(P_TOPK):
        mx = jnp.max(x, axis=0, keepdims=True)
        am = jnp.min(jnp.where(x == mx, ids, big), axis=0, keepdims=True)
        vals.append(mx)
        idxs.append(am)
        x = jnp.where(ids == am, -jnp.inf, x)
    return jnp.concatenate(vals, axis=0), jnp.concatenate(idxs, axis=0)


def _pair_candidates(a, b):
    t = a.shape[1]
    row16 = lax.broadcasted_iota(jnp.int32, (P_TOPK, t), 0)
    row8 = lax.broadcasted_iota(jnp.int32, (SUBLANES, t), 0)
    vals = [a[0:1, :] + b]
    ids = [row16]
    for i in range(1, SUBLANES):
        vals.append(a[i:i + 1, :] + b[0:SUBLANES, :])
        ids.append(row8 + P_TOPK * i)
    vals.append(a[SUBLANES:, :] + b[0:1, :])
    ids.append((row8 + SUBLANES) * P_TOPK)
    return jnp.concatenate(vals, axis=0), jnp.concatenate(ids, axis=0)


def _pick(table, sel):
    out = jnp.zeros_like(table)
    for a in range(P_TOPK):
        out = jnp.where(sel == a, table[a:a + 1, :], out)
    return out


def _topk_kernel(pq_ref, sk_ref, idx_ref, g_ref):
    pq = pq_ref[...]
    key_ids = lax.broadcasted_iota(jnp.int32, (P_KEYS, pq.shape[0]), 0)
    for h in range(P_HEADS):
        sv, si = [], []
        for p in range(2):
            j = 2 * h + p
            s = _dot_nt(sk_ref[j], pq[:, j * P_HALF:(j + 1) * P_HALF])
            v, i = _top16(s, key_ids)
            sv.append(v)
            si.append(i)
        fv, fi = _top16(*_pair_candidates(sv[0], sv[1]))
        e = _pick(si[0], fi >> 4) * P_KEYS + _pick(si[1], fi & (P_TOPK - 1))
        ex = jnp.exp(fv - fv[0:1, :])
        g = ex / jnp.sum(ex, axis=0, keepdims=True)
        idx_ref[h * P_TOPK:(h + 1) * P_TOPK, :] = e
        g_ref[h * P_TOPK:(h + 1) * P_TOPK, :] = g


def _peer_topk(pq, sub_keys, tt):
    t = pq.shape[0]
    return pl.pallas_call(
        _topk_kernel,
        grid=(t // tt,),
        in_specs=[pl.BlockSpec((tt, D_MODEL), lambda i: (i, 0)), _resident(sub_keys.shape)],
        out_specs=[pl.BlockSpec((P_SEL, tt), lambda i: (0, i)), pl.BlockSpec((P_SEL, tt), lambda i: (0, i))],
        out_shape=[jax.ShapeDtypeStruct((P_SEL, t), jnp.int32), jax.ShapeDtypeStruct((P_SEL, t), F32)],
        compiler_params=pltpu.CompilerParams(dimension_semantics=("arbitrary",), vmem_limit_bytes=VMEM_LIMIT),
        name="peer_topk",
    )(pq, sub_keys)


PEER_SUB = 16
WORD_ROWS = D_MODEL // (2 * LANES)


def _pack_table(t):
    e, d = t.shape
    return t.astype(BF16).reshape(e, d // LANES, LANES)


def _pack_table_words(t):
    e, d = t.shape
    bits = lax.bitcast_convert_type(t.astype(BF16), jnp.uint16).astype(jnp.uint32)
    bits = bits.reshape(e, d // (2 * LANES), 2, LANES)
    return (bits[:, :, 0] | (bits[:, :, 1] << 16)).reshape(e * (d // (2 * LANES)), LANES)


def _merge_sublanes(a, b, step):
    sub = lax.broadcasted_iota(jnp.int32, (SUBLANES, LANES), 0)
    low = (sub & step) == 0
    if 2 * step == SUBLANES:
        return jnp.where(low, a, b) + pltpu.roll(jnp.where(low, b, a), step, axis=0)
    return jnp.where(low, a + pltpu.roll(a, SUBLANES - step, axis=0), b + pltpu.roll(b, step, axis=0))


def _fold8(parts):
    quads = [_merge_sublanes(parts[i], parts[i + 4], 4) for i in range(4)]
    even = _merge_sublanes(quads[0], quads[2], 2)
    odd = _merge_sublanes(quads[1], quads[3], 2)
    return _merge_sublanes(even, odd, 1)


def _id_pipeline(e_hbm, ebuf, sem, body, carry):
    step = pl.program_id(0)
    words = PEER_SUB * P_SEL
    nsub = LANES // PEER_SUB
    total = pl.num_programs(0) * nsub

    def copy(sub, slot):
        return pltpu.make_async_copy(e_hbm.at[pl.ds(sub * words, words)], ebuf.at[slot], sem.at[slot])

    @pl.when(step == 0)
    def _():
        copy(0, 0).start()

    def pair(j, carry):
        sub = step * nsub + 2 * j
        copy(sub, 0).wait()
        copy(sub + 1, 1).start()
        carry = body(0, 2 * j * PEER_SUB, carry)

        @pl.when(sub + 2 < total)
        def _():
            copy(sub + 2, 0).start()
        copy(sub + 1, 1).wait()
        return body(1, (2 * j + 1) * PEER_SUB, carry)
    return lax.fori_loop(0, nsub // 2, pair, carry)


def _peer_act_kernel(e_hbm, x_ref, g_ref, tab_ref, w_ref, q_ref, ebuf, sem, *, tb):
    def gather(slot, tok0, carry):
        for k in range(PEER_SUB):
            x = x_ref[tok0 + k]
            for grp in range(P_SEL // SUBLANES):
                parts = [tab_ref[ebuf[slot, k * P_SEL + grp * SUBLANES + j]].astype(F32) * x
                         for j in range(SUBLANES)]
                q_ref[tok0 + k, grp * SUBLANES:(grp + 1) * SUBLANES, :] = _fold8(parts)
        return carry
    _id_pipeline(e_hbm, ebuf, sem, gather, 0)

    kk = lax.broadcasted_iota(jnp.int32, (2 * LANES, 2 * LANES), 0)
    nn = lax.broadcasted_iota(jnp.int32, (2 * LANES, 2 * LANES), 1)
    ones = jnp.where((kk < LANES) == (nn < LANES), 1.0, 0.0).astype(BF16)
    lane = lax.broadcasted_iota(jnp.int32, (P_SEL, LANES), 1)
    act = jnp.zeros((P_SEL, tb), F32)
    for t in range(0, tb, 2):
        qh, ql = _split2(jnp.concatenate([q_ref[t], q_ref[t + 1]], axis=1))
        r = _dot(qh, ones) + _dot(ql, ones)
        act = jnp.where(lane == t, r[:, 0:LANES], act)
        act = jnp.where(lane == t + 1, r[:, LANES:], act)
    gelu = 0.5 * act * (1.0 + lax.erf(act * (2.0 ** -0.5)))
    w_ref[...] = g_ref[...] * gelu


def _id_scratch():
    return [pltpu.SMEM((2, PEER_SUB * P_SEL), jnp.int32), pltpu.SemaphoreType.DMA((2,))]


def _peer_act(e_flat, x3, g_t, table, tb):
    t = x3.shape[0]
    assert tb == LANES and t % tb == 0
    return pl.pallas_call(
        functools.partial(_peer_act_kernel, tb=tb),
        grid=(t // tb,),
        in_specs=[pl.BlockSpec(memory_space=pltpu.HBM),
                  pl.BlockSpec((tb, SUBLANES, LANES), lambda i: (i, 0, 0)),
                  pl.BlockSpec((P_SEL, tb), lambda i: (0, i)),
                  _resident(table.shape)],
        out_specs=pl.BlockSpec((P_SEL, tb), lambda i: (0, i)),
        out_shape=jax.ShapeDtypeStruct((P_SEL, t), F32),
        scratch_shapes=[pltpu.VMEM((tb, P_SEL, LANES), F32)] + _id_scratch(),
        compiler_params=pltpu.CompilerParams(dimension_semantics=("arbitrary",),
                                             vmem_limit_bytes=TABLE_VMEM_LIMIT),
        name="peer_act",
    )(e_flat, x3, g_t, table)


def _peer_out_kernel(e_hbm, w_ref, x1_ref, gt_ref, tab_ref, o_ref, wb_ref, wsp_ref, y_ref, ebuf, sem, *, tb):
    wh, wl = _split2(w_ref[...])
    wsp_ref[0] = wh
    wsp_ref[1] = wl
    rows = lax.broadcasted_iota(jnp.int32, (tb, 2 * LANES), 0)
    second = lax.broadcasted_iota(jnp.int32, (tb, 2 * LANES), 1) >= LANES

    def spread(slot, t0, k):
        onehot = jnp.where(rows == jnp.where(second, t0 + k + 1, t0 + k), 1.0, 0.0).astype(BF16)
        both = _dot(wsp_ref[0], onehot) + _dot(wsp_ref[1], onehot)
        wb_ref[slot, k] = both[:, 0:LANES]
        wb_ref[slot, k + 1] = both[:, LANES:]

    for k in range(0, PEER_SUB, 2):
        spread(0, 0, k)

    def gather(slot, tok0, carry):
        for k in range(PEER_SUB):
            acc = [jnp.zeros((SUBLANES, LANES), F32), jnp.zeros((SUBLANES, LANES), F32)]
            for c in range(P_SEL):
                first = pl.multiple_of(ebuf[slot, k * P_SEL + c], WORD_ROWS)
                row = pltpu.bitcast(tab_ref[pl.ds(first, WORD_ROWS), :], BF16).astype(F32)
                w = jnp.broadcast_to(wb_ref[slot, k, c:c + 1, :], (SUBLANES, LANES))
                acc[c % 2] = acc[c % 2] + row * w
            y_ref[pl.ds(pl.multiple_of((tok0 + k) * SUBLANES, SUBLANES), SUBLANES), :] = acc[0] + acc[1]
            if k % 2 == 1:
                spread(1 - slot, tok0 + PEER_SUB, k - 1)
        return carry
    _id_pipeline(e_hbm, ebuf, sem, gather, 0)

    gate = gt_ref[0] if len(gt_ref.shape) == 3 else gt_ref[...]
    for s in range(SUBLANES):
        cols = slice(s * LANES, (s + 1) * LANES)
        o_ref[:, cols] = x1_ref[:, cols] + gate[:, cols] * y_ref[pl.ds(s, tb, stride=SUBLANES), :]


def _peer_out(e_flat, w_t, x1, gt, table, tb):
    t = x1.shape[0]
    g = gt.shape[0]
    assert tb == LANES and t % tb == 0
    if g == t:
        gt_spec = pl.BlockSpec((tb, D_MODEL), lambda i: (i, 0))
    else:
        per = t // g
        assert per % tb == 0
        gt = gt.reshape(g, 1, D_MODEL)
        gt_spec = pl.BlockSpec((1, 1, D_MODEL), lambda i: ((i * tb) // per, 0, 0))
    return pl.pallas_call(
        functools.partial(_peer_out_kernel, tb=tb),
        grid=(t // tb,),
        in_specs=[pl.BlockSpec(memory_space=pltpu.HBM),
                  pl.BlockSpec((P_SEL, tb), lambda i: (0, i)),
                  pl.BlockSpec((tb, D_MODEL), lambda i: (i, 0)),
                  gt_spec,
                  _resident(table.shape)],
        out_specs=pl.BlockSpec((tb, D_MODEL), lambda i: (i, 0)),
        out_shape=jax.ShapeDtypeStruct((t, D_MODEL), F32),
        scratch_shapes=[pltpu.VMEM((2, PEER_SUB, P_SEL, LANES), F32), pltpu.VMEM((2, P_SEL, tb), BF16),
                        pltpu.VMEM((tb * SUBLANES, LANES), F32)] + _id_scratch(),
        compiler_params=pltpu.CompilerParams(dimension_semantics=("arbitrary",),
                                             vmem_limit_bytes=TABLE_VMEM_LIMIT),
        name="peer_out",
    )(e_flat, w_t, x1, gt, table)


def _prep_weights(w_in, b_in, g_qa, g_ka):
    sizes = (A_WIDTH, A_WIDTH, A_WIDTH, A_HEADS, B_WIDTH, B_WIDTH, B_WIDTH, B_HEADS, B_HEADS, B_WIDTH,
             D_MODEL, D_MODEL)
    offs = [0]
    for s in sizes:
        offs.append(offs[-1] + s)
    col = lambda a, i: a[..., offs[i]:offs[i + 1]]
    cat = lambda a, ids: jnp.concatenate([col(a, i) for i in ids], axis=-1)
    wa, ba = cat(w_in, (0, 1, 2)), cat(b_in, (0, 1, 2))
    wb, bb = cat(w_in, (4, 5, 6)), cat(b_in, (4, 5, 6))
    wg, bg = cat(w_in, (9, 10, 11)), cat(b_in, (9, 10, 11))
    ws, bs = cat(w_in, (3, 7, 8)), cat(b_in, (3, 7, 8))
    pad = SMALL_W - ws.shape[-1]
    ws = jnp.pad(ws, ((0, 0), (0, pad)))
    bs = jnp.pad(bs, ((0, pad),))
    wsh, wsl = _split2(ws)
    blk = lax.broadcasted_iota(jnp.int32, (A_WIDTH, A_WIDTH), 0) // A_HEAD_DIM
    blk_t = lax.broadcasted_iota(jnp.int32, (A_WIDTH, A_WIDTH), 1) // A_HEAD_DIM
    bd = jnp.where(blk == blk_t, 1.0 / A_HEAD_DIM, 0.0).astype(BF16)
    r = lambda v: v.reshape(1, -1)
    return (wa.astype(BF16), r(ba), wb.astype(BF16), r(bb), wg.astype(BF16), r(bg), wsh, wsl, r(bs),
            r(g_qa), r(g_ka), bd)


def _gate_rows(small):
    return jnp.swapaxes(small[..., 0:16], 1, 2)


def kernel(x_prompt, x_sample, c_prompt, c_sample, cache_k, cache_v, cache_logf, state_C, state_n, state_m,
           w_mod, b_mod, g_norm1, g_norm2, w_in, b_in, g_qa, g_ka, g_hb, w_pa, w_pb, w_o, w_pq, sub_keys,
           expert_u, expert_v):
    assert w_mod.shape[0] == 1, "single-layer step"
    bp, sp, _ = x_prompt.shape
    bs, ss, _ = x_sample.shape
    past = cache_k.shape[2]
    inw = _prep_weights(w_in[0], b_in[0], g_qa[0], g_ka[0])
    g1, g2 = g_norm1[0].reshape(1, -1), g_norm2[0].reshape(1, -1)
    ghb = g_hb[0].reshape(1, -1)
    post_w = (w_pa[0].astype(BF16), w_pb[0].astype(BF16), w_o[0].astype(BF16), w_pq[0].astype(BF16))
    skeys = sub_keys[0].reshape(2 * P_HEADS, P_KEYS, P_HALF).astype(BF16)
    tab_u, tab_v = _pack_table(expert_u[0]), _pack_table_words(expert_v[0])

    mod_p = _modulation(c_prompt, w_mod[0], b_mod[0])[:, None, :]
    mod_s = _modulation(c_sample, w_mod[0], b_mod[0])[:, None, :]

    tm = 256
    (qa, ka, va, qb, kb, vb, ob, sga, sgb, small) = _inproj(x_prompt, mod_p, g1, inw, tm)
    srow = _gate_rows(small)
    fk = _cumsum_lanes(srow[:, 0:A_HEADS, :]).reshape(bp, A_HEADS // 2, 2, sp)
    ha = _fox_attention(qa, ka, va, fk, tq=min(1024, sp), tk=min(1024, sp), qoff=0)
    zc = jnp.zeros((bp, B_HEADS, B_DIM, B_DIM), F32)
    zn = jnp.zeros((bp, B_HEADS, 1, B_DIM), F32)
    hb, c_p, n_p, m_p = _mlstm(qb, kb, vb, small, srow, zc, zn, zn, chunk=128)
    x1_p, h2_p, pq_p = _postmix(ha, hb, ob, sga, sgb, x_prompt, mod_p, ghb, g2, *post_w, tm)
    state_p = (ka.reshape(1, bp, sp, A_HEADS, A_HEAD_DIM), va.reshape(1, bp, sp, A_HEADS, A_HEAD_DIM),
               small[None, :, :, 0:A_HEADS], c_p[None], n_p.reshape(1, bp, B_HEADS, B_DIM),
               m_p[None, :, :, 0, 0])

    (qa, ka, va, qb, kb, vb, ob, sga, sgb, small) = _inproj(x_sample, mod_s, g1, inw, ss)
    s_all = past + ss
    s_pad = -(-s_all // LANES) * LANES
    padk = lambda a: jnp.pad(a, ((0, 0), (0, s_pad - s_all), (0, 0)))
    kk = padk(jnp.concatenate([cache_k[0].reshape(bs, past, A_WIDTH), ka], axis=1))
    vv = padk(jnp.concatenate([cache_v[0].reshape(bs, past, A_WIDTH), va], axis=1))
    lf_all = padk(jnp.concatenate([cache_logf[0], small[:, :, 0:A_HEADS]], axis=1))
    fk = _cumsum_lanes(jnp.swapaxes(lf_all, 1, 2)).reshape(bs, A_HEADS // 2, 2, s_pad)
    ha = _fox_attention(qa, kk, vv, fk, tq=ss, tk=s_pad, qoff=past)
    chunk = 128
    padt = lambda a: jnp.pad(a, ((0, 0), (0, chunk - ss), (0, 0)))
    colid = jnp.arange(SMALL_W)
    is_input = (colid >= A_HEADS) & (colid < A_HEADS + B_HEADS)
    real = (jnp.arange(chunk) < ss)[None, :, None]
    small_pad = jnp.where(real, padt(small), jnp.where(is_input, NEG, 0.0)[None, None, :])
    c0 = state_C[0].astype(F32)
    n0 = state_n[0].astype(F32).reshape(bs, B_HEADS, 1, B_DIM)
    m0 = jnp.broadcast_to(state_m[0].astype(F32)[:, :, None, None], (bs, B_HEADS, 1, LANES))
    hb, c_s, n_s, m_s = _mlstm(padt(qb), padt(kb), padt(vb), small_pad, _gate_rows(small_pad), c0, n0, m0,
                               chunk=chunk)
    x1_s, h2_s, pq_s = _postmix(ha, hb[:, 0:ss], ob, sga, sgb, x_sample, mod_s, ghb, g2, *post_w, ss)
    state_s = (ka.reshape(1, bs, ss, A_HEADS, A_HEAD_DIM), va.reshape(1, bs, ss, A_HEADS, A_HEAD_DIM),
               small[None, :, :, 0:A_HEADS], c_s[None], n_s.reshape(1, bs, B_HEADS, B_DIM),
               m_s[None, :, :, 0, 0])

    def channel_mix(x1, h2_tiles, pq, gt):
        t = x1.shape[0] * x1.shape[1]
        tb = LANES
        assert t % tb == 0
        idx_t, g_t = _peer_topk(pq.reshape(t, D_MODEL), skeys, tb)
        e_flat = jnp.swapaxes(idx_t, 0, 1).reshape(-1)
        w_t = _peer_act(e_flat, h2_tiles.reshape(t, SUBLANES, LANES), g_t, tab_u, tb)
        return _peer_out(e_flat * WORD_ROWS, w_t, x1.reshape(t, D_MODEL), gt, tab_v, tb).reshape(x1.shape)

    gt2 = lambda mod, rep: jnp.repeat(mod[:, 0, 5 * D_MODEL:6 * D_MODEL], rep, axis=0)
    tp, ts = bp * sp, bs * ss
    both = lambda a, b, w: jnp.concatenate([a.reshape(-1, w), b.reshape(-1, w)], axis=0)[None]
    y = channel_mix(both(x1_p, x1_s, D_MODEL), both(h2_p, h2_s, LANES), both(pq_p, pq_s, D_MODEL),
                    jnp.concatenate([gt2(mod_p, sp), gt2(mod_s, ss)], axis=0))[0]
    y_p = y[0:tp].reshape(bp, sp, D_MODEL)
    y_s = y[tp:].reshape(bs, ss, D_MODEL)
    return (y_p, y_s) + state_p + state_s
```
